```python
import math
import jax
import jax.numpy as jnp
from jax import lax
import numpy as np

D_MODEL = 2048
BATCH = 4
SEQ = 4096
DEPTH = 2

GRID_W = 64
N_HEADS = 16
HEAD_DIM = D_MODEL // N_HEADS
WIN_H = 8
WIN_W = 16
S5_GROUP = 16
S5_GROUPS = D_MODEL // S5_GROUP
S5_STATE = 64
D_FF = ((8 * D_MODEL // 3 + 127) // 128) * 128
N_MIXERS = 2
N_A = (DEPTH + 1) // 2
N_B = DEPTH // 2
ALPHA = (2 * DEPTH) ** 0.25
BETA = (8 * DEPTH) ** -0.25
LN_EPS = 1e-5
MIN_NEG_RE = -1e-4

kernel_name = "hybrid_s5_natten_macaron_deepnorm"


def layer_norm(x, g, b):
    xf = x.astype(jnp.float32)
    mu = jnp.mean(xf, axis=-1, keepdims=True)
    xc = xf - mu
    var = jnp.mean(xc * xc, axis=-1, keepdims=True)
    y = xc * lax.rsqrt(var + LN_EPS) * g.astype(jnp.float32) + b.astype(jnp.float32)
    return y.astype(x.dtype)


def swiglu(x, w_gate, w_up, w_down):
    return (jax.nn.silu(x @ w_gate) * (x @ w_up)) @ w_down


def _complex_combine(left, right):
    a1r, a1i, b1r, b1i = left
    a2r, a2i, b2r, b2i = right
    ar = a2r * a1r - a2i * a1i
    ai = a2r * a1i + a2i * a1r
    br = a2r * b1r - a2i * b1i + b2r
    bi = a2r * b1i + a2i * b1r + b2i
    return (ar, ai, br, bi)


def s5_direction(u, lam_re, lam_im, log_dt, b_re, b_im, c_re, c_im, reverse):
    L = u.shape[1]
    lr = jnp.minimum(lam_re.astype(jnp.float32), MIN_NEG_RE)
    li = lam_im.astype(jnp.float32)
    dt = jnp.exp(log_dt.astype(jnp.float32))[:, None]
    mag = jnp.exp(lr * dt)
    lb_re = mag * jnp.cos(li * dt)
    lb_im = mag * jnp.sin(li * dt)
    den = lr * lr + li * li
    nr = lb_re - 1.0
    ni = lb_im
    f_re = (nr * lr + ni * li) / den
    f_im = (ni * lr - nr * li) / den
    br = b_re.astype(jnp.float32)
    bi = b_im.astype(jnp.float32)
    bb_re = f_re[:, :, None] * br - f_im[:, :, None] * bi
    bb_im = f_re[:, :, None] * bi + f_im[:, :, None] * br
    bu_re = jnp.einsum('blgc,gpc->blgp', u, bb_re)
    bu_im = jnp.einsum('blgc,gpc->blgp', u, bb_im)
    a_re = jnp.broadcast_to(lb_re[None, None], (1, L) + lb_re.shape)
    a_im = jnp.broadcast_to(lb_im[None, None], (1, L) + lb_im.shape)
    _, _, s_re, s_im = lax.associative_scan(
        _complex_combine, (a_re, a_im, bu_re, bu_im), reverse=reverse, axis=1)
    return (jnp.einsum('blgp,gcp->blgc', s_re, c_re.astype(jnp.float32))
            - jnp.einsum('blgp,gcp->blgc', s_im, c_im.astype(jnp.float32)))


def s5_mixer(x, lam_re, lam_im, log_dt, b_re, b_im, c_re, c_im, d_skip, w_val, w_gate):
    Bsz, L, D = x.shape
    u = x.astype(jnp.float32).reshape(Bsz, L, S5_GROUPS, S5_GROUP)
    y_f = s5_direction(u, lam_re[0], lam_im[0], log_dt[0], b_re[0], b_im[0],
                       c_re[0], c_im[0], reverse=False)
    y_b = s5_direction(u, lam_re[1], lam_im[1], log_dt[1], b_re[1], b_im[1],
                       c_re[1], c_im[1], reverse=True)
    y = (y_f + y_b).reshape(Bsz, L, D) + d_skip.astype(jnp.float32) * x.astype(jnp.float32)
    g = jax.nn.gelu(y).astype(x.dtype)
    return (g @ w_val) * jax.nn.sigmoid(g @ w_gate)


def natten_mixer(x, w_qkv, rpb, w_out):
    Bsz, L, D = x.shape
    rows = L // GRID_W
    kh = min(WIN_H, rows)
    qkv = (x @ w_qkv).reshape(Bsz, rows, GRID_W, 3, N_HEADS, HEAD_DIM)
    q = qkv[:, :, :, 0] * (HEAD_DIM ** -0.5)
    k = qkv[:, :, :, 1]
    v = qkv[:, :, :, 2]
    cols = np.arange(GRID_W)
    col_start = np.clip(cols - WIN_W // 2, 0, GRID_W - WIN_W)
    col_idx = col_start[:, None] + np.arange(WIN_W)[None, :]
    col_off = col_idx - cols[:, None] + (WIN_W - 1)
    win_rows = jnp.arange(kh)

    def one_row(r):
        rs = jnp.clip(r - kh // 2, 0, rows - kh)
        k_rows = lax.dynamic_slice_in_dim(k, rs, kh, axis=1)
        v_rows = lax.dynamic_slice_in_dim(v, rs, kh, axis=1)
        k_win = k_rows[:, :, col_idx]
        v_win = v_rows[:, :, col_idx]
        q_r = lax.dynamic_index_in_dim(q, r, axis=1, keepdims=False)
        s = jnp.einsum('bchd,bicjhd->bhcij', q_r, k_win).astype(jnp.float32)
        row_off = rs + win_rows - r + (WIN_H - 1)
        bias = rpb[:, row_off][:, :, col_off]
        s = s + jnp.transpose(bias, (0, 2, 1, 3)).astype(jnp.float32)[None]
        p = jax.nn.softmax(s.reshape(Bsz, N_HEADS, GRID_W, kh * WIN_W), axis=-1)
        p = p.reshape(Bsz, N_HEADS, GRID_W, kh, WIN_W).astype(v.dtype)
        return jnp.einsum('bhcij,bicjhd->bchd', p, v_win)

    out = lax.map(one_row, jnp.arange(rows))
    out = jnp.moveaxis(out, 0, 1).reshape(Bsz, L, D)
    return out @ w_out


def setup_inputs(seed: int = 0) -> dict:
    key = jax.random.key(seed)
    ks = jax.random.split(key, 20)
    D, F, G, P, GC = D_MODEL, D_FF, S5_GROUPS, S5_STATE, S5_GROUP
    x = jax.random.normal(ks[0], (BATCH, SEQ, D), jnp.float32)
    ffn_w_gate = jax.random.normal(ks[1], (DEPTH, 2, D, F), jnp.float32) * D ** -0.5
    ffn_w_up = jax.random.normal(ks[2], (DEPTH, 2, D, F), jnp.float32) * D ** -0.5
    ffn_w_down = jax.random.normal(ks[3], (DEPTH, 2, F, D), jnp.float32) * (F ** -0.5 * BETA)
    ln_g = 1.0 + 0.02 * jax.random.normal(ks[4], (DEPTH, 3, D), jnp.float32)
    ln_b = 0.02 * jax.random.normal(ks[5], (DEPTH, 3, D), jnp.float32)
    s5_lam_re = -0.5 + 0.01 * jax.random.normal(ks[6], (N_A, 2, G, P), jnp.float32)
    n_idx = jnp.arange(P, dtype=jnp.float32)
    s5_lam_im = math.pi * n_idx + 0.01 * jax.random.normal(ks[7], (N_A, 2, G, P), jnp.float32)
    s5_log_dt = jax.random.uniform(ks[8], (N_A, 2, G), jnp.float32,
                                   minval=math.log(1e-3), maxval=math.log(1e-1))
    s5_b_re = jax.random.normal(ks[9], (N_A, 2, G, P, GC), jnp.float32) * (2 * GC) ** -0.5
    s5_b_im = jax.random.normal(ks[10], (N_A, 2, G, P, GC), jnp.float32) * (2 * GC) ** -0.5
    s5_c_re = jax.random.normal(ks[11], (N_A, 2, G, GC, P), jnp.float32) * (2 * P) ** -0.5
    s5_c_im = jax.random.normal(ks[12], (N_A, 2, G, GC, P), jnp.float32) * (2 * P) ** -0.5
    s5_d = jax.random.normal(ks[13], (N_A, D), jnp.float32)
    s5_w_glu_val = jax.random.normal(ks[14], (N_A, D, D), jnp.float32) * (D ** -0.5 * BETA)
    s5_w_glu_gate = jax.random.normal(ks[15], (N_A, D, D), jnp.float32) * D ** -0.5
    na_w_qkv = jax.random.normal(ks[16], (N_B, D, 3 * D), jnp.float32) * D ** -0.5
    na_rpb = 0.02 * jax.random.normal(ks[17], (N_B, N_HEADS, 2 * WIN_H - 1, 2 * WIN_W - 1), jnp.float32)
    na_w_out = jax.random.normal(ks[18], (N_B, D, D), jnp.float32) * (D ** -0.5 * BETA)
    return {"x": x, "ffn_w_gate": ffn_w_gate, "ffn_w_up": ffn_w_up, "ffn_w_down": ffn_w_down,
            "ln_g": ln_g, "ln_b": ln_b,
            "s5_lam_re": s5_lam_re, "s5_lam_im": s5_lam_im, "s5_log_dt": s5_log_dt,
            "s5_b_re": s5_b_re, "s5_b_im": s5_b_im, "s5_c_re": s5_c_re, "s5_c_im": s5_c_im,
            "s5_d": s5_d, "s5_w_glu_val": s5_w_glu_val, "s5_w_glu_gate": s5_w_glu_gate,
            "na_w_qkv": na_w_qkv, "na_rpb": na_rpb, "na_w_out": na_w_out}


def reference(x, ffn_w_gate, ffn_w_up, ffn_w_down, ln_g, ln_b,
              s5_lam_re, s5_lam_im, s5_log_dt, s5_b_re, s5_b_im, s5_c_re, s5_c_im,
              s5_d, s5_w_glu_val, s5_w_glu_gate, na_w_qkv, na_rpb, na_w_out):
    for i in range(DEPTH):
        f1 = swiglu(x, ffn_w_gate[i, 0], ffn_w_up[i, 0], ffn_w_down[i, 0])
        x = layer_norm(ALPHA * x + 0.5 * f1, ln_g[i, 0], ln_b[i, 0])
        j = i // N_MIXERS
        if i % N_MIXERS == 0:
            m = s5_mixer(x, s5_lam_re[j], s5_lam_im[j], s5_log_dt[j], s5_b_re[j], s5_b_im[j],
                         s5_c_re[j], s5_c_im[j], s5_d[j], s5_w_glu_val[j], s5_w_glu_gate[j])
        else:
            m = natten_mixer(x, na_w_qkv[j], na_rpb[j], na_w_out[j])
        x = layer_norm(ALPHA * x + m, ln_g[i, 1], ln_b[i, 1])
        f2 = swiglu(x, ffn_w_gate[i, 1], ffn_w_up[i, 1], ffn_w_down[i, 1])
        x = layer_norm(ALPHA * x + 0.5 * f2, ln_g[i, 2], ln_b[i, 2])
    return x
```

```python
import functools
import math

import jax
import jax.numpy as jnp
import numpy as np
from jax import lax
from jax.experimental import pallas as pl
from jax.experimental.pallas import tpu as pltpu

GRID_W = 64
N_HEADS = 16
WIN_H = 8
WIN_W = 16
S5_GROUP = 16
S5_STATE = 64
N_MIXERS = 2
LN_EPS = 1e-5
MIN_NEG_RE = -1e-4

LANES = 128
VMEM_BYTES_V7X = 64 * 1024 * 1024
VMEM_LIMIT_BYTES = VMEM_BYTES_V7X - 8 * 1024 * 1024

S5_CHUNK = 16
MASK_VALUE = -1e30

F32 = jnp.float32
BF16 = jnp.bfloat16


def _compiler_params(n_axes):
    return pltpu.CompilerParams(dimension_semantics=("arbitrary",) * n_axes,
                                vmem_limit_bytes=VMEM_LIMIT_BYTES)


def _layer_norm_rows(z, g, b):
    mu = jnp.mean(z, axis=-1, keepdims=True)
    zc = z - mu
    var = jnp.mean(zc * zc, axis=-1, keepdims=True)
    return zc * lax.rsqrt(var + LN_EPS) * g + b


def _ffn_ln_kernel(x_ref, wg_ref, wu_ref, wd_ref, g_ref, b_ref, o_ref, xb_ref, acc_ref, *, nf, alpha):
    f = pl.program_id(1)

    @pl.when(f == 0)
    def _():
        xb_ref[...] = x_ref[...].astype(BF16)

    xb = xb_ref[...]
    gate = jnp.dot(xb, wg_ref[...], preferred_element_type=F32)
    up = jnp.dot(xb, wu_ref[...], preferred_element_type=F32)
    h = (gate * jax.nn.sigmoid(gate) * up).astype(BF16)
    part = jnp.dot(h, wd_ref[...], preferred_element_type=F32)

    @pl.when(f == 0)
    def _():
        acc_ref[...] = part

    @pl.when(f > 0)
    def _():
        acc_ref[...] += part

    @pl.when(f == nf - 1)
    def _():
        z = alpha * x_ref[...] + 0.5 * acc_ref[...]
        o_ref[...] = _layer_norm_rows(z, g_ref[...], b_ref[...])


def _ffn_ln(x, wg, wu, wd, g, b, *, alpha, tm, tf):
    m, d = x.shape
    fp = wg.shape[1]
    nf = fp // tf
    return pl.pallas_call(
        functools.partial(_ffn_ln_kernel, nf=nf, alpha=alpha),
        grid=(m // tm, nf),
        in_specs=[
            pl.BlockSpec((tm, d), lambda i, f: (i, 0)),
            pl.BlockSpec((d, tf), lambda i, f: (0, f)),
            pl.BlockSpec((d, tf), lambda i, f: (0, f)),
            pl.BlockSpec((tf, d), lambda i, f: (f, 0)),
            pl.BlockSpec((1, d), lambda i, f: (0, 0)),
            pl.BlockSpec((1, d), lambda i, f: (0, 0)),
        ],
        out_specs=pl.BlockSpec((tm, d), lambda i, f: (i, 0)),
        out_shape=jax.ShapeDtypeStruct((m, d), F32),
        scratch_shapes=[pltpu.VMEM((tm, d), BF16), pltpu.VMEM((tm, d), F32)],
        compiler_params=_compiler_params(2),
        name="ffn_ln",
    )(x, wg, wu, wd, g, b)


def _s5_kernel(u_ref, w1_ref, w2_ref, a1_ref, a2_ref, y_ref, *, n_chunks):
    width = u_ref.shape[-1]
    n_state = (w1_ref.shape[-1] - width) // 2
    half = n_state // 2
    u = u_ref[0]
    r1 = jnp.dot(u, w1_ref[0], preferred_element_type=F32)
    y_intra = r1[:, :width]
    s_fwd = r1[:, width:width + n_state]
    s_bwd = r1[:, width + n_state:]
    rows = s_fwd.shape[0]
    j = lax.broadcasted_iota(jnp.int32, s_fwd.shape, 0) & (n_chunks - 1)

    def cmul(a1, a2, x):
        return a1 * x + a2 * pltpu.roll(x, half, axis=1)

    n_steps = n_chunks.bit_length() - 1
    xf = jnp.where(j >= 1, pltpu.roll(s_fwd, 1, axis=0), 0.0)
    xb = jnp.where(j < n_chunks - 1, pltpu.roll(s_bwd, rows - 1, axis=0), 0.0)
    for m in range(n_steps):
        sh = 1 << m
        sf = jnp.where(j >= sh, pltpu.roll(xf, sh, axis=0), 0.0)
        xf = xf + cmul(a1_ref[0, 0, m:m + 1, :], a2_ref[0, 0, m:m + 1, :], sf)
        sb = jnp.where(j < n_chunks - sh, pltpu.roll(xb, rows - sh, axis=0), 0.0)
        xb = xb + cmul(a1_ref[0, 1, m:m + 1, :], a2_ref[0, 1, m:m + 1, :], sb)
    s_in = jnp.concatenate([xf, xb], axis=1).astype(BF16)
    y_ref[0] = y_intra + jnp.dot(s_in, w2_ref[0], preferred_element_type=F32)


def _s5_ssm(ug, w1, w2, a1, a2, *, n_chunks):
    n_groups, rows, width = ug.shape
    return pl.pallas_call(
        functools.partial(_s5_kernel, n_chunks=n_chunks),
        grid=(n_groups,),
        in_specs=[
            pl.BlockSpec((1, rows, width), lambda g: (g, 0, 0)),
            pl.BlockSpec((1,) + w1.shape[1:], lambda g: (g, 0, 0)),
            pl.BlockSpec((1,) + w2.shape[1:], lambda g: (g, 0, 0)),
            pl.BlockSpec((1,) + a1.shape[1:], lambda g: (g, 0, 0, 0)),
            pl.BlockSpec((1,) + a2.shape[1:], lambda g: (g, 0, 0, 0)),
        ],
        out_specs=pl.BlockSpec((1, rows, width), lambda g: (g, 0, 0)),
        out_shape=jax.ShapeDtypeStruct((n_groups, rows, width), F32),
        compiler_params=_compiler_params(1),
        name="s5_ssm",
    )(ug, w1, w2, a1, a2)


def _s5_tables(lam_re, lam_im, log_dt, b_re, b_im, c_re, c_im, n_chunks):
    hi = lax.Precision.HIGHEST
    t = S5_CHUNK
    n_groups, p = lam_re.shape[1:]
    gc = b_re.shape[-1]
    lr = jnp.minimum(lam_re.astype(F32), MIN_NEG_RE)
    li = lam_im.astype(F32)
    dt = jnp.exp(log_dt.astype(F32))[..., None]
    mag = jnp.exp(lr * dt)
    lb_re = mag * jnp.cos(li * dt)
    lb_im = mag * jnp.sin(li * dt)
    den = lr * lr + li * li
    nr = lb_re - 1.0
    ni = lb_im
    f_re = (nr * lr + ni * li) / den
    f_im = (ni * lr - nr * li) / den
    br = b_re.astype(F32)
    bi = b_im.astype(F32)
    bb_re = f_re[..., None] * br - f_im[..., None] * bi
    bb_im = f_re[..., None] * bi + f_im[..., None] * br
    cr = c_re.astype(F32)
    ci = c_im.astype(F32)

    pw_re = [jnp.ones_like(lb_re)]
    pw_im = [jnp.zeros_like(lb_im)]
    for _ in range(t):
        r_, i_ = pw_re[-1], pw_im[-1]
        pw_re.append(r_ * lb_re - i_ * lb_im)
        pw_im.append(r_ * lb_im + i_ * lb_re)
    pw_re = jnp.stack(pw_re, axis=2)
    pw_im = jnp.stack(pw_im, axis=2)

    cp_re = cr[:, :, None] * pw_re[:, :, :, None, :] - ci[:, :, None] * pw_im[:, :, :, None, :]
    cp_im = cr[:, :, None] * pw_im[:, :, :, None, :] + ci[:, :, None] * pw_re[:, :, :, None, :]
    kern = (jnp.einsum('dgkcp,dgpe->dgkce', cp_re[:, :, :t], bb_re, precision=hi)
            - jnp.einsum('dgkcp,dgpe->dgkce', cp_im[:, :, :t], bb_im, precision=hi))

    sig = np.arange(t)[:, None]
    tau = np.arange(t)[None, :]
    lag = tau - sig
    kf = jnp.where((lag >= 0)[None, :, :, None, None], kern[0][:, np.clip(lag, 0, t - 1)], 0.0)
    kb = jnp.where((lag <= 0)[None, :, :, None, None], kern[1][:, np.clip(-lag, 0, t - 1)], 0.0)
    toep = jnp.transpose(kf + kb, (0, 1, 4, 2, 3)).reshape(n_groups, t * gc, t * gc)

    def state_in(d, powers):
        pr = pw_re[d][:, powers][..., None]
        pi = pw_im[d][:, powers][..., None]
        re = pr * bb_re[d][:, None] - pi * bb_im[d][:, None]
        im = pr * bb_im[d][:, None] + pi * bb_re[d][:, None]
        m = jnp.concatenate([re, im], axis=2)
        return jnp.transpose(m, (0, 1, 3, 2)).reshape(n_groups, t * gc, 2 * p)

    def state_out(d, powers):
        re = cp_re[d][:, powers]
        im = -cp_im[d][:, powers]
        m = jnp.concatenate([re, im], axis=3)
        return jnp.transpose(m, (0, 3, 1, 2)).reshape(n_groups, 2 * p, t * gc)

    steps = np.arange(t)
    w1 = jnp.concatenate([toep, state_in(0, t - 1 - steps), state_in(1, steps)], axis=2)
    w2 = jnp.concatenate([state_out(0, steps + 1), state_out(1, t - steps)], axis=1)

    a_re, a_im = pw_re[:, :, t], pw_im[:, :, t]
    a1, a2 = [], []
    for _ in range(n_chunks.bit_length() - 1):
        a1.append(jnp.concatenate([a_re, a_re], axis=-1))
        a2.append(jnp.concatenate([-a_im, a_im], axis=-1))
        a_re, a_im = a_re * a_re - a_im * a_im, 2.0 * a_re * a_im
    a1 = jnp.transpose(jnp.stack(a1, axis=2), (1, 0, 2, 3))
    a2 = jnp.transpose(jnp.stack(a2, axis=2), (1, 0, 2, 3))
    return w1.astype(BF16), w2.astype(BF16), a1, a2


def _glu_ln_kernel(y_ref, x_ref, d_ref, wv_ref, wg_ref, g_ref, b_ref, o_ref, act_ref, *, nn, tn, alpha):
    n = pl.program_id(1)

    @pl.when(n == 0)
    def _():
        yy = y_ref[...] + d_ref[...] * x_ref[...]
        act_ref[...] = jax.nn.gelu(yy).astype(BF16)

    a = act_ref[...]
    val = jnp.dot(a, wv_ref[...], preferred_element_type=F32)
    gate = jnp.dot(a, wg_ref[...], preferred_element_type=F32)
    o_ref[:, pl.ds(pl.multiple_of(n * tn, tn), tn)] = val * jax.nn.sigmoid(gate)

    @pl.when(n == nn - 1)
    def _():
        z = alpha * x_ref[...] + o_ref[...]
        o_ref[...] = _layer_norm_rows(z, g_ref[...], b_ref[...])


def _glu_ln(y, x, d_skip, wv, wg, g, b, *, alpha, tm, tn):
    m, d = x.shape
    nn = d // tn
    return pl.pallas_call(
        functools.partial(_glu_ln_kernel, nn=nn, tn=tn, alpha=alpha),
        grid=(m // tm, nn),
        in_specs=[
            pl.BlockSpec((tm, d), lambda i, n: (i, 0)),
            pl.BlockSpec((tm, d), lambda i, n: (i, 0)),
            pl.BlockSpec((1, d), lambda i, n: (0, 0)),
            pl.BlockSpec((d, tn), lambda i, n: (0, n)),
            pl.BlockSpec((d, tn), lambda i, n: (0, n)),
            pl.BlockSpec((1, d), lambda i, n: (0, 0)),
            pl.BlockSpec((1, d), lambda i, n: (0, 0)),
        ],
        out_specs=pl.BlockSpec((tm, d), lambda i, n: (i, 0)),
        out_shape=jax.ShapeDtypeStruct((m, d), F32),
        scratch_shapes=[pltpu.VMEM((tm, d), BF16)],
        compiler_params=_compiler_params(2),
        name="glu_ln",
    )(y, x, d_skip, wv, wg, g, b)


def _qkv_kernel(x_ref, w_ref, o_ref, xb_ref, *, n_q_tiles, scale):
    n = pl.program_id(1)

    @pl.when(n == 0)
    def _():
        xb_ref[...] = x_ref[...].astype(BF16)

    r = jnp.dot(xb_ref[...], w_ref[...], preferred_element_type=F32)
    r = r * jnp.where(n < n_q_tiles, scale, 1.0)
    o_ref[...] = r.astype(BF16)


def _qkv_proj(x, w, *, d_q, scale, tm, tn):
    m, d = x.shape
    n_out = w.shape[1]
    return pl.pallas_call(
        functools.partial(_qkv_kernel, n_q_tiles=d_q // tn, scale=scale),
        grid=(m // tm, n_out // tn),
        in_specs=[
            pl.BlockSpec((tm, d), lambda i, n: (i, 0)),
            pl.BlockSpec((d, tn), lambda i, n: (0, n)),
        ],
        out_specs=pl.BlockSpec((tm, tn), lambda i, n: (i, n)),
        out_shape=jax.ShapeDtypeStruct((m, n_out), BF16),
        scratch_shapes=[pltpu.VMEM((tm, d), BF16)],
        compiler_params=_compiler_params(2),
        name="qkv_proj",
    )(x, w)


def _natten_kernel(q_ref, k_ref, v_ref, bias_ref, o_ref, *, grid_rows, kh):
    w = GRID_W

    def one_row(r, carry):
        rs = jnp.clip(r - kh // 2, 0, grid_rows - kh)
        row_off = rs - r + (WIN_H - 1)
        q = q_ref[0, pl.ds(pl.multiple_of(r * w, w), w), :]
        k = k_ref[0, pl.ds(pl.multiple_of(rs * w, w), kh * w), :]
        v = v_ref[0, pl.ds(pl.multiple_of(rs * w, w), kh * w), :]
        s = lax.dot_general(q, k, (((1,), (1,)), ((), ())), preferred_element_type=F32)
        s = s + bias_ref[0, row_off]
        s_max = jnp.max(s, axis=-1, keepdims=True)
        p = jnp.exp(s - s_max)
        denom = jnp.sum(p, axis=-1, keepdims=True)
        o = jnp.dot(p.astype(BF16), v, preferred_element_type=F32) / denom
        o_ref[0, pl.ds(pl.multiple_of(r * w, w), w), :] = o.astype(BF16)
        return carry

    lax.fori_loop(0, grid_rows, one_row, 0)


def _natten(qkv, bias, *, kh):
    bsz, seq, three_d = qkv.shape
    n_heads = bias.shape[0]
    dh = three_d // (3 * n_heads)
    grid_rows = seq // GRID_W
    return pl.pallas_call(
        functools.partial(_natten_kernel, grid_rows=grid_rows, kh=kh),
        grid=(n_heads, bsz),
        in_specs=[
            pl.BlockSpec((1, seq, dh), lambda h, b: (b, 0, h)),
            pl.BlockSpec((1, seq, dh), lambda h, b: (b, 0, n_heads + h)),
            pl.BlockSpec((1, seq, dh), lambda h, b: (b, 0, 2 * n_heads + h)),
            pl.BlockSpec((1,) + bias.shape[1:], lambda h, b: (h, 0, 0, 0)),
        ],
        out_specs=pl.BlockSpec((1, seq, dh), lambda h, b: (b, 0, h)),
        out_shape=jax.ShapeDtypeStruct((bsz, seq, n_heads * dh), BF16),
        compiler_params=_compiler_params(2),
        name="natten",
    )(qkv, qkv, qkv, bias)


def _natten_bias(rpb, kh):
    cols = np.arange(GRID_W)
    col_start = np.clip(cols - WIN_W // 2, 0, GRID_W - WIN_W)
    kc = np.arange(GRID_W)
    in_win = (kc[None, :] >= col_start[:, None]) & (kc[None, :] < col_start[:, None] + WIN_W)
    col_off = np.clip(kc[None, :] - cols[:, None] + (WIN_W - 1), 0, 2 * WIN_W - 2)
    cb = jnp.where(in_win[None, None], rpb.astype(F32)[:, :, col_off], MASK_VALUE)
    n_off = 2 * WIN_H - kh
    rows = np.arange(n_off)[:, None] + np.arange(kh)[None, :]
    b = cb[:, rows]
    b = jnp.transpose(b, (0, 1, 3, 2, 4))
    return b.reshape(b.shape[0], n_off, GRID_W, kh * GRID_W)


def _proj_ln_kernel(a_ref, x_ref, w_ref, g_ref, b_ref, o_ref, *, nn, tn, alpha):
    n = pl.program_id(1)
    o_ref[:, pl.ds(pl.multiple_of(n * tn, tn), tn)] = jnp.dot(
        a_ref[...], w_ref[...], preferred_element_type=F32)

    @pl.when(n == nn - 1)
    def _():
        z = alpha * x_ref[...] + o_ref[...]
        o_ref[...] = _layer_norm_rows(z, g_ref[...], b_ref[...])


def _proj_ln(a, x, w, g, b, *, alpha, tm, tn):
    m, d = x.shape
    nn = d // tn
    return pl.pallas_call(
        functools.partial(_proj_ln_kernel, nn=nn, tn=tn, alpha=alpha),
        grid=(m // tm, nn),
        in_specs=[
            pl.BlockSpec((tm, a.shape[1]), lambda i, n: (i, 0)),
            pl.BlockSpec((tm, d), lambda i, n: (i, 0)),
            pl.BlockSpec((a.shape[1], tn), lambda i, n: (0, n)),
            pl.BlockSpec((1, d), lambda i, n: (0, 0)),
            pl.BlockSpec((1, d), lambda i, n: (0, 0)),
        ],
        out_specs=pl.BlockSpec((tm, d), lambda i, n: (i, 0)),
        out_shape=jax.ShapeDtypeStruct((m, d), F32),
        compiler_params=_compiler_params(2),
        name="proj_ln",
    )(a, x, w, g, b)


def _tiles(m):
    tm = 512 if m % 512 == 0 else m
    return dict(tm=tm, tf=512, tn=512)


def _pad_cols(w, mult):
    pad = (-w.shape[-1]) % mult
    return jnp.pad(w, ((0, 0), (0, pad))) if pad else w


def _pad_rows(w, mult):
    pad = (-w.shape[0]) % mult
    return jnp.pad(w, ((0, pad), (0, 0))) if pad else w


def kernel(x, ffn_w_gate, ffn_w_up, ffn_w_down, ln_g, ln_b, s5_lam_re, s5_lam_im, s5_log_dt, s5_b_re, s5_b_im, s5_c_re, s5_c_im, s5_d, s5_w_glu_val, s5_w_glu_gate, na_w_qkv, na_rpb, na_w_out):
    bsz, seq, d = x.shape
    depth = ffn_w_gate.shape[0]
    alpha = (2 * depth) ** 0.25
    m = bsz * seq
    t = _tiles(m)
    tm, tf, tn = t["tm"], t["tf"], t["tn"]
    n_groups = d // S5_GROUP
    n_chunks = seq // S5_CHUNK
    grid_rows = seq // GRID_W
    kh = min(WIN_H, grid_rows)
    dh = d // N_HEADS

    def ffn(h, i, k):
        wg = _pad_cols(ffn_w_gate[i, k].astype(BF16), tf)
        wu = _pad_cols(ffn_w_up[i, k].astype(BF16), tf)
        wd = _pad_rows(ffn_w_down[i, k].astype(BF16), tf)
        return _ffn_ln(h, wg, wu, wd, ln_g[i, k * 2][None], ln_b[i, k * 2][None], alpha=alpha, tm=tm, tf=tf)

    h = x.reshape(m, d)
    for i in range(depth):
        h = ffn(h, i, 0)
        jm = i // N_MIXERS
        g1, b1 = ln_g[i, 1][None], ln_b[i, 1][None]
        if i % N_MIXERS == 0:
            w1, w2, a1, a2 = _s5_tables(s5_lam_re[jm], s5_lam_im[jm], s5_log_dt[jm], s5_b_re[jm], s5_b_im[jm],
                                        s5_c_re[jm], s5_c_im[jm], n_chunks)
            ug = h.astype(BF16).reshape(bsz, n_chunks, S5_CHUNK, n_groups, S5_GROUP)
            ug = jnp.transpose(ug, (3, 0, 1, 2, 4)).reshape(n_groups, bsz * n_chunks, S5_CHUNK * S5_GROUP)
            yg = _s5_ssm(ug, w1, w2, a1, a2, n_chunks=n_chunks)
            yg = yg.reshape(n_groups, bsz, n_chunks, S5_CHUNK, S5_GROUP)
            y = jnp.transpose(yg, (1, 2, 3, 0, 4)).reshape(m, d)
            h = _glu_ln(y, h, s5_d[jm][None], s5_w_glu_val[jm].astype(BF16), s5_w_glu_gate[jm].astype(BF16),
                        g1, b1, alpha=alpha, tm=tm, tn=tn)
        else:
            qkv = _qkv_proj(h, na_w_qkv[jm].astype(BF16), d_q=d, scale=dh ** -0.5, tm=tm, tn=tn)
            att = _natten(qkv.reshape(bsz, seq, 3 * d), _natten_bias(na_rpb[jm], kh), kh=kh)
            h = _proj_ln(att.reshape(m, d), h, na_w_out[jm].astype(BF16), g1, b1, alpha=alpha, tm=tm, tn=tn)
        h = ffn(h, i, 1)
    return h.reshape(bsz, seq, d)
```

```python
import functools
import math

import jax
import jax.numpy as jnp
import numpy as np
from jax import lax
from jax.experimental import pallas as pl
from jax.experimental.pallas import tpu as pltpu

GRID_W = 64
N_HEADS = 16
WIN_H = 8
WIN_W = 16
S5_GROUP = 16
N_MIXERS = 2
LN_EPS = 1e-5
MIN_NEG_RE = -1e-4

LANES = 128
VMEM_BYTES_V7X = 64 * 1024 * 1024
VMEM_LIMIT_BYTES = VMEM_BYTES_V7X - 8 * 1024 * 1024

S5_CHUNK = 16
MASK_VALUE = -1e30

F32 = jnp.float32
BF16 = jnp.bfloat16
HIGHEST = lax.Precision.HIGHEST
NT_DIMS = (((1,), (1,)), ((), ()))


def _compiler_params(n_axes):
    return pltpu.CompilerParams(dimension_semantics=("arbitrary",) * n_axes,
                                vmem_limit_bytes=VMEM_LIMIT_BYTES)


def _layer_norm_rows(z, g, b):
    mu = jnp.mean(z, axis=-1, keepdims=True)
    zc = z - mu
    var = jnp.mean(zc * zc, axis=-1, keepdims=True)
    return zc * lax.rsqrt(var + LN_EPS) * g + b


def _ffn_ln_kernel(x_ref, wg_ref, wu_ref, wd_ref, g_ref, b_ref, o_ref, xb_ref, acc_ref, *, nf, alpha):
    f = pl.program_id(1)

    @pl.when(f == 0)
    def _():
        xb_ref[...] = x_ref[...].astype(BF16)
        acc_ref[...] = jnp.zeros_like(acc_ref)

    xb = xb_ref[...]
    gate = jnp.dot(xb, wg_ref[...], preferred_element_type=F32)
    up = jnp.dot(xb, wu_ref[...], preferred_element_type=F32)
    h = (gate * jax.nn.sigmoid(gate) * up).astype(BF16)
    acc_ref[...] += jnp.dot(h, wd_ref[...], preferred_element_type=F32)

    @pl.when(f == nf - 1)
    def _():
        z = alpha * x_ref[...] + acc_ref[...]
        o_ref[...] = _layer_norm_rows(z, g_ref[...], b_ref[...])


def _ffn_ln(x, wg, wu, wd, g, b, layer, k, *, alpha, tm, tf):
    m, d = x.shape
    nf = wg.shape[-1] // tf
    return pl.pallas_call(
        functools.partial(_ffn_ln_kernel, nf=nf, alpha=alpha),
        grid=(m // tm, nf),
        in_specs=[
            pl.BlockSpec((tm, d), lambda i, f: (i, 0)),
            pl.BlockSpec((None, None, d, tf), lambda i, f: (layer, k, 0, f)),
            pl.BlockSpec((None, None, d, tf), lambda i, f: (layer, k, 0, f)),
            pl.BlockSpec((None, None, tf, d), lambda i, f: (layer, k, f, 0)),
            pl.BlockSpec((None, None, 1, d), lambda i, f: (layer, 2 * k, 0, 0)),
            pl.BlockSpec((None, None, 1, d), lambda i, f: (layer, 2 * k, 0, 0)),
        ],
        out_specs=pl.BlockSpec((tm, d), lambda i, f: (i, 0)),
        out_shape=jax.ShapeDtypeStruct((m, d), F32),
        scratch_shapes=[pltpu.VMEM((tm, d), BF16), pltpu.VMEM((tm, d), F32)],
        compiler_params=_compiler_params(2),
        name="ffn_ln",
    )(x, wg, wu, wd, g, b)


def _s5_kernel(*refs, n_chunks, n_steps):
    x_refs = refs[:n_steps]
    p_ref, w1_ref, w2_ref, a1_ref, a2_ref, y_ref = refs[n_steps:]
    groups = w1_ref.shape[0]
    width = w1_ref.shape[1]
    n_state = (w1_ref.shape[2] - width) // 2
    half = n_state // 2
    rows = y_ref.shape[1]
    n_scan = n_chunks.bit_length() - 1

    ucat = jnp.concatenate([x_ref[...].astype(BF16) for x_ref in x_refs], axis=1)
    uperm = jnp.dot(ucat, p_ref[...], preferred_element_type=F32).astype(BF16)
    j = lax.broadcasted_iota(jnp.int32, (rows, n_state), 0) & (n_chunks - 1)

    def cmul(a1, a2, v):
        return a1 * v + a2 * pltpu.roll(v, half, axis=1)

    ys = []
    for gl in range(groups):
        u = uperm[:, gl * width:(gl + 1) * width]
        r1 = jnp.dot(u, w1_ref[gl], preferred_element_type=F32)
        s_fwd = r1[:, width:width + n_state]
        s_bwd = r1[:, width + n_state:]
        xf = jnp.where(j >= 1, pltpu.roll(s_fwd, 1, axis=0), 0.0)
        xb = jnp.where(j < n_chunks - 1, pltpu.roll(s_bwd, rows - 1, axis=0), 0.0)
        for m in range(n_scan):
            sh = 1 << m
            sf = jnp.where(j >= sh, pltpu.roll(xf, sh, axis=0), 0.0)
            xf = xf + cmul(a1_ref[gl, 0, m:m + 1, :], a2_ref[gl, 0, m:m + 1, :], sf)
            sb = jnp.where(j < n_chunks - sh, pltpu.roll(xb, rows - sh, axis=0), 0.0)
            xb = xb + cmul(a1_ref[gl, 1, m:m + 1, :], a2_ref[gl, 1, m:m + 1, :], sb)
        s_in = jnp.concatenate([xf, xb], axis=1).astype(BF16)
        y = r1[:, :width] + lax.dot_general(s_in, w2_ref[gl], NT_DIMS, preferred_element_type=F32)
        ys.append(y.astype(BF16))
    ycat = jnp.concatenate(ys, axis=1)
    yperm = lax.dot_general(ycat, p_ref[...], NT_DIMS, preferred_element_type=F32)
    for s in range(n_steps):
        y_ref[s] = yperm[:, s * LANES:(s + 1) * LANES]


def _s5_ssm(x2, perm, w1, w2, a1, a2, *, n_chunks, n_steps, rows):
    n_rows = x2.shape[0]
    d = x2.shape[1] // n_steps
    gpt = LANES // S5_GROUP

    def x_spec(s):
        return pl.BlockSpec((rows, LANES), lambda t, r: (r, s * (d // LANES) + t))

    return pl.pallas_call(
        functools.partial(_s5_kernel, n_chunks=n_chunks, n_steps=n_steps),
        grid=(d // LANES, n_rows // rows),
        in_specs=[x_spec(s) for s in range(n_steps)] + [
            pl.BlockSpec(perm.shape, lambda t, r: (0, 0)),
            pl.BlockSpec((gpt,) + w1.shape[1:], lambda t, r: (t, 0, 0)),
            pl.BlockSpec((gpt,) + w2.shape[1:], lambda t, r: (t, 0, 0)),
            pl.BlockSpec((gpt,) + a1.shape[1:], lambda t, r: (t, 0, 0, 0)),
            pl.BlockSpec((gpt,) + a2.shape[1:], lambda t, r: (t, 0, 0, 0)),
        ],
        out_specs=pl.BlockSpec((n_steps, rows, LANES), lambda t, r: (0, r, t)),
        out_shape=jax.ShapeDtypeStruct((n_steps, n_rows, d), F32),
        compiler_params=_compiler_params(2),
        name="s5_ssm",
    )(*([x2] * n_steps), perm, w1, w2, a1, a2)


def _s5_lane_shuffle():
    gpt = LANES // S5_GROUP
    n = S5_CHUNK * LANES
    src = np.arange(n)
    step, grp, ch = src // LANES, (src % LANES) // S5_GROUP, src % S5_GROUP
    dst = grp * (S5_CHUNK * S5_GROUP) + step * S5_GROUP + ch
    assert gpt * S5_CHUNK * S5_GROUP == n
    col = lax.broadcasted_iota(jnp.int32, (n, n), 1)
    return (col == jnp.asarray(dst, jnp.int32)[:, None]).astype(BF16)


def _s5_tables(lam_re, lam_im, log_dt, b_re, b_im, c_re, c_im, n_chunks):
    t = S5_CHUNK
    n_groups, p = lam_re.shape[1:]
    gc = b_re.shape[-1]
    lr = jnp.minimum(lam_re.astype(F32), MIN_NEG_RE)
    li = lam_im.astype(F32)
    dt = jnp.exp(log_dt.astype(F32))[..., None]
    x, th = lr * dt, li * dt
    mag = jnp.exp(x)
    lb_re, lb_im = mag * jnp.cos(th), mag * jnp.sin(th)
    den = lr * lr + li * li
    nr, ni = lb_re - 1.0, lb_im
    f_re = (nr * lr + ni * li) / den
    f_im = (ni * lr - nr * li) / den
    brt = jnp.swapaxes(b_re.astype(F32), -1, -2)
    bit = jnp.swapaxes(b_im.astype(F32), -1, -2)
    bb_re = f_re[:, :, None] * brt - f_im[:, :, None] * bit
    bb_im = f_re[:, :, None] * bit + f_im[:, :, None] * brt
    cr, ci = c_re.astype(F32), c_im.astype(F32)

    kk = jnp.arange(t + 1, dtype=F32)[None, None, :, None, None]
    pmag = jnp.exp(kk * x[:, :, None, None, :])
    pw_re = pmag * jnp.cos(kk * th[:, :, None, None, :])
    pw_im = pmag * jnp.sin(kk * th[:, :, None, None, :])

    cp = jnp.concatenate([cr[:, :, None] * pw_re - ci[:, :, None] * pw_im,
                          -(cr[:, :, None] * pw_im + ci[:, :, None] * pw_re)], axis=-1)
    pb = jnp.concatenate([pw_re * bb_re[:, :, None] - pw_im * bb_im[:, :, None],
                          pw_re * bb_im[:, :, None] + pw_im * bb_re[:, :, None]], axis=-1)
    kern = jnp.einsum('dgkcp,dgep->dgkce', cp[:, :, :t], pb[:, :, 0], precision=HIGHEST)
    kern = jnp.concatenate([kern[0], kern[1]], axis=1)

    sig = np.arange(t)[:, None]
    tau = np.arange(t)[None, :]
    sel = np.zeros((2 * t, t, t), np.float32)
    for k in range(t):
        sel[k] = (tau - sig == k)
        sel[t + k] = (sig - tau == k)
    toep = jnp.einsum('kst,gkce->gsetc', jnp.asarray(sel), kern, precision=HIGHEST)
    toep = toep.reshape(n_groups, t * gc, t * gc)

    sin_f = pb[0][:, t - 1::-1][:, :t].reshape(n_groups, t * gc, 2 * p)
    sin_b = pb[1][:, :t].reshape(n_groups, t * gc, 2 * p)
    w1 = jnp.concatenate([toep, sin_f, sin_b], axis=2)
    out_f = cp[0][:, 1:t + 1].reshape(n_groups, t * gc, 2 * p)
    out_b = cp[1][:, t:0:-1].reshape(n_groups, t * gc, 2 * p)
    w2 = jnp.concatenate([out_f, out_b], axis=2)

    a_re, a_im = pw_re[:, :, t, 0], pw_im[:, :, t, 0]
    a1, a2 = [], []
    for _ in range(n_chunks.bit_length() - 1):
        a1.append(jnp.concatenate([a_re, a_re], axis=-1))
        a2.append(jnp.concatenate([-a_im, a_im], axis=-1))
        a_re, a_im = a_re * a_re - a_im * a_im, 2.0 * a_re * a_im
    a1 = jnp.transpose(jnp.stack(a1, axis=2), (1, 0, 2, 3))
    a2 = jnp.transpose(jnp.stack(a2, axis=2), (1, 0, 2, 3))
    return w1.astype(BF16), w2.astype(BF16), a1, a2


def _glu_ln_kernel(y_ref, x_ref, d_ref, wv_ref, wg_ref, g_ref, b_ref, o_ref, act_ref, *, nn, tn, alpha):
    n = pl.program_id(2)

    @pl.when(n == 0)
    def _():
        yy = y_ref[...] + d_ref[...] * x_ref[...]
        act_ref[...] = jax.nn.gelu(yy).astype(BF16)

    a = act_ref[...]
    val = jnp.dot(a, wv_ref[...], preferred_element_type=F32)
    gate = jnp.dot(a, wg_ref[...], preferred_element_type=F32)
    o_ref[:, pl.ds(pl.multiple_of(n * tn, tn), tn)] = val * jax.nn.sigmoid(gate)

    @pl.when(n == nn - 1)
    def _():
        z = alpha * x_ref[...] + o_ref[...]
        o_ref[...] = _layer_norm_rows(z, g_ref[...], b_ref[...])


def _glu_ln(y, x2, d_skip, wv, wg, g, b, *, alpha, tm, tn):
    n_steps, n_rows, d = y.shape
    nn = d // tn
    return pl.pallas_call(
        functools.partial(_glu_ln_kernel, nn=nn, tn=tn, alpha=alpha),
        grid=(n_steps, n_rows // tm, nn),
        in_specs=[
            pl.BlockSpec((None, tm, d), lambda s, i, n: (s, i, 0)),
            pl.BlockSpec((tm, d), lambda s, i, n: (i, s)),
            pl.BlockSpec((1, d), lambda s, i, n: (0, 0)),
            pl.BlockSpec((d, tn), lambda s, i, n: (0, n)),
            pl.BlockSpec((d, tn), lambda s, i, n: (0, n)),
            pl.BlockSpec((1, d), lambda s, i, n: (0, 0)),
            pl.BlockSpec((1, d), lambda s, i, n: (0, 0)),
        ],
        out_specs=pl.BlockSpec((tm, d), lambda s, i, n: (i, s)),
        out_shape=jax.ShapeDtypeStruct(x2.shape, F32),
        scratch_shapes=[pltpu.VMEM((tm, d), BF16)],
        compiler_params=_compiler_params(3),
        name="glu_ln",
    )(y, x2, d_skip, wv, wg, g, b)


def _qkv_kernel(x_ref, w_ref, o_ref, xb_ref, *, n_q_tiles, scale):
    n = pl.program_id(1)

    @pl.when(n == 0)
    def _():
        xb_ref[...] = x_ref[...].astype(BF16)

    r = jnp.dot(xb_ref[...], w_ref[...], preferred_element_type=F32)
    r = r * jnp.where(n < n_q_tiles, scale, 1.0)
    o_ref[...] = r.astype(BF16)


def _qkv_proj(x, w, *, d_q, scale, tm, tn):
    m, d = x.shape
    n_out = w.shape[1]
    return pl.pallas_call(
        functools.partial(_qkv_kernel, n_q_tiles=d_q // tn, scale=scale),
        grid=(m // tm, n_out // tn),
        in_specs=[
            pl.BlockSpec((tm, d), lambda i, n: (i, 0)),
            pl.BlockSpec((d, tn), lambda i, n: (0, n)),
        ],
        out_specs=pl.BlockSpec((tm, tn), lambda i, n: (i, n)),
        out_shape=jax.ShapeDtypeStruct((m, n_out), BF16),
        scratch_shapes=[pltpu.VMEM((tm, d), BF16)],
        compiler_params=_compiler_params(2),
        name="qkv_proj",
    )(x, w)


NATTEN_ROWS_PER_ITER = 8


def _natten_kernel(q_ref, k_ref, v_ref, bias_ref, o_ref, *, grid_rows, kh, rows_per_iter):
    w = GRID_W

    def row_block(rb, carry):
        pending = []
        for u in range(rows_per_iter):
            r = rb * rows_per_iter + u
            rs = jnp.clip(r - kh // 2, 0, grid_rows - kh)
            q = q_ref[0, pl.ds(pl.multiple_of(r * w, w), w), :]
            k = k_ref[0, pl.ds(pl.multiple_of(rs * w, w), kh * w), :]
            s = lax.dot_general(q, k, NT_DIMS, preferred_element_type=F32)
            pending.append((r, rs, s + bias_ref[0, rs - r + (WIN_H - 1)]))
        for r, rs, s in pending:
            v = v_ref[0, pl.ds(pl.multiple_of(rs * w, w), kh * w), :]
            p = jnp.exp(s - jnp.max(s, axis=-1, keepdims=True))
            denom = jnp.sum(p, axis=-1, keepdims=True)
            o = jnp.dot(p.astype(BF16), v, preferred_element_type=F32) / denom
            o_ref[0, pl.ds(pl.multiple_of(r * w, w), w), :] = o.astype(BF16)
        return carry

    lax.fori_loop(0, grid_rows // rows_per_iter, row_block, 0)


def _natten(qkv, bias, *, kh):
    bsz, seq, three_d = qkv.shape
    n_heads = bias.shape[0]
    dh = three_d // (3 * n_heads)
    grid_rows = seq // GRID_W
    rows_per_iter = math.gcd(grid_rows, NATTEN_ROWS_PER_ITER)
    return pl.pallas_call(
        functools.partial(_natten_kernel, grid_rows=grid_rows, kh=kh, rows_per_iter=rows_per_iter),
        grid=(n_heads, bsz),
        in_specs=[
            pl.BlockSpec((1, seq, dh), lambda h, b: (b, 0, h)),
            pl.BlockSpec((1, seq, dh), lambda h, b: (b, 0, n_heads + h)),
            pl.BlockSpec((1, seq, dh), lambda h, b: (b, 0, 2 * n_heads + h)),
            pl.BlockSpec((1,) + bias.shape[1:], lambda h, b: (h, 0, 0, 0)),
        ],
        out_specs=pl.BlockSpec((1, seq, dh), lambda h, b: (b, 0, h)),
        out_shape=jax.ShapeDtypeStruct((bsz, seq, n_heads * dh), BF16),
        compiler_params=_compiler_params(2),
        name="natten",
    )(qkv, qkv, qkv, bias)


def _natten_bias(rpb, kh):
    cols = np.arange(GRID_W)
    col_start = np.clip(cols - WIN_W // 2, 0, GRID_W - WIN_W)
    kc = np.arange(GRID_W)
    in_win = (kc[None, :] >= col_start[:, None]) & (kc[None, :] < col_start[:, None] + WIN_W)
    col_off = kc[None, :] - cols[:, None] + (WIN_W - 1)
    onehot = ((col_off[None] == np.arange(2 * WIN_W - 1)[:, None, None]) & in_win[None]).astype(np.float32)
    n_off = 2 * WIN_H - kh
    r2 = jnp.stack([rpb.astype(F32)[:, o:o + kh] for o in range(n_off)], axis=1)
    b = jnp.einsum('hriw,wck->hrcik', r2, jnp.asarray(onehot), precision=HIGHEST)
    b = b + jnp.asarray(np.where(in_win, 0.0, MASK_VALUE), F32)[None, None, :, None, :]
    return b.reshape(b.shape[0], n_off, GRID_W, kh * GRID_W)


def _proj_ln_kernel(a_ref, x_ref, w_ref, g_ref, b_ref, o_ref, *, nn, tn, alpha):
    n = pl.program_id(1)
    o_ref[:, pl.ds(pl.multiple_of(n * tn, tn), tn)] = jnp.dot(
        a_ref[...], w_ref[...], preferred_element_type=F32)

    @pl.when(n == nn - 1)
    def _():
        z = alpha * x_ref[...] + o_ref[...]
        o_ref[...] = _layer_norm_rows(z, g_ref[...], b_ref[...])


def _proj_ln(a, x, w, g, b, *, alpha, tm, tn):
    m, d = x.shape
    nn = d // tn
    return pl.pallas_call(
        functools.partial(_proj_ln_kernel, nn=nn, tn=tn, alpha=alpha),
        grid=(m // tm, nn),
        in_specs=[
            pl.BlockSpec((tm, a.shape[1]), lambda i, n: (i, 0)),
            pl.BlockSpec((tm, d), lambda i, n: (i, 0)),
            pl.BlockSpec((a.shape[1], tn), lambda i, n: (0, n)),
            pl.BlockSpec((1, d), lambda i, n: (0, 0)),
            pl.BlockSpec((1, d), lambda i, n: (0, 0)),
        ],
        out_specs=pl.BlockSpec((tm, d), lambda i, n: (i, 0)),
        out_shape=jax.ShapeDtypeStruct((m, d), F32),
        compiler_params=_compiler_params(2),
        name="proj_ln",
    )(a, x, w, g, b)


def _tiles(m, d, bsz, n_chunks):
    tm = 512 if m % 512 == 0 else m
    s5_rows = n_chunks * max(1, min(bsz, 512 // n_chunks))
    while (bsz * n_chunks) % s5_rows:
        s5_rows -= n_chunks
    glu_tm = math.gcd(bsz * n_chunks, 512)
    return dict(tm=tm, ffn_tf=512, glu_tm=glu_tm, glu_tn=min(d, 1024), qkv_tn=d, proj_tn=d, s5_rows=s5_rows)


def _pad_to(w, axis, mult):
    pad = (-w.shape[axis]) % mult
    if not pad:
        return w
    widths = [(0, 0)] * w.ndim
    widths[axis] = (0, pad)
    return jnp.pad(w, widths)


def kernel(x, ffn_w_gate, ffn_w_up, ffn_w_down, ln_g, ln_b, s5_lam_re, s5_lam_im, s5_log_dt, s5_b_re, s5_b_im, s5_c_re, s5_c_im, s5_d, s5_w_glu_val, s5_w_glu_gate, na_w_qkv, na_rpb, na_w_out):
    bsz, seq, d = x.shape
    depth = ffn_w_gate.shape[0]
    alpha = (2 * depth) ** 0.25
    m = bsz * seq
    n_chunks = seq // S5_CHUNK
    t = _tiles(m, d, bsz, n_chunks)
    tm = t["tm"]
    grid_rows = seq // GRID_W
    kh = min(WIN_H, grid_rows)
    dh = d // N_HEADS

    wg_all = _pad_to(ffn_w_gate.astype(BF16), 3, t["ffn_tf"])
    wu_all = _pad_to(ffn_w_up.astype(BF16), 3, t["ffn_tf"])
    wd_all = _pad_to((0.5 * ffn_w_down).astype(BF16), 2, t["ffn_tf"])
    g_all = ln_g[:, :, None, :]
    b_all = ln_b[:, :, None, :]

    h = x.reshape(m, d)
    for i in range(depth):
        h = _ffn_ln(h, wg_all, wu_all, wd_all, g_all, b_all, i, 0, alpha=alpha, tm=tm, tf=t["ffn_tf"])
        jm = i // N_MIXERS
        g1, b1 = ln_g[i, 1][None], ln_b[i, 1][None]
        if i % N_MIXERS == 0:
            w1, w2, a1, a2 = _s5_tables(s5_lam_re[jm], s5_lam_im[jm], s5_log_dt[jm], s5_b_re[jm], s5_b_im[jm],
                                        s5_c_re[jm], s5_c_im[jm], n_chunks)
            h2 = h.reshape(bsz * n_chunks, S5_CHUNK * d)
            y = _s5_ssm(h2, _s5_lane_shuffle(), w1, w2, a1, a2, n_chunks=n_chunks, n_steps=S5_CHUNK,
                        rows=t["s5_rows"])
            h = _glu_ln(y, h2, s5_d[jm][None], s5_w_glu_val[jm].astype(BF16), s5_w_glu_gate[jm].astype(BF16),
                        g1, b1, alpha=alpha, tm=t["glu_tm"], tn=t["glu_tn"]).reshape(m, d)
        else:
            qkv = _qkv_proj(h, na_w_qkv[jm].astype(BF16), d_q=d, scale=dh ** -0.5, tm=tm, tn=t["qkv_tn"])
            att = _natten(qkv.reshape(bsz, seq, 3 * d), _natten_bias(na_rpb[jm], kh), kh=kh)
            h = _proj_ln(att.reshape(m, d), h, na_w_out[jm].astype(BF16), g1, b1, alpha=alpha, tm=tm,
                         tn=t["proj_tn"])
        h = _ffn_ln(h, wg_all, wu_all, wd_all, g_all, b_all, i, 1, alpha=alpha, tm=tm, tf=t["ffn_tf"])
    return h.reshape(bsz, seq, d)
```

```python
import functools
import math

import jax
import jax.numpy as jnp
import numpy as np
from jax import lax
from jax.experimental import pallas as pl
from jax.experimental.pallas import tpu as pltpu

GRID_W = 64
N_HEADS = 16
WIN_H = 8
WIN_W = 16
S5_GROUP = 16
N_MIXERS = 2
LN_EPS = 1e-5
MIN_NEG_RE = -1e-4

LANES = 128
VMEM_BYTES_V7X = 64 * 1024 * 1024
VMEM_LIMIT_BYTES = VMEM_BYTES_V7X - 2 * 1024 * 1024

FFN_SUB_ROWS = 512
EPILOGUE_SUB_ROWS = 256
S5_CHUNK = 16
S5_SUB_TOKENS = 512
S5_TOKENS_PER_STEP = 8192
MASK_VALUE = -1e30

F32 = jnp.float32
BF16 = jnp.bfloat16
HIGHEST = lax.Precision.HIGHEST
NT_DIMS = (((1,), (1,)), ((), ()))


def _compiler_params(n_axes):
    return pltpu.CompilerParams(dimension_semantics=("arbitrary",) * n_axes,
                                vmem_limit_bytes=VMEM_LIMIT_BYTES)


def _layer_norm_rows(z, g, b):
    mu = jnp.mean(z, axis=-1, keepdims=True)
    zc = z - mu
    var = jnp.mean(zc * zc, axis=-1, keepdims=True)
    return zc * lax.rsqrt(var + LN_EPS) * g + b


def _ffn_ln_kernel(x_ref, wg_ref, wu_ref, wd_ref, g_ref, b_ref, o_ref, xb_ref, *, nf, tail, alpha):
    f = pl.program_id(1)
    tm = o_ref.shape[0]
    sub = math.gcd(tm, FFN_SUB_ROWS)

    @pl.when(f == 0)
    def _():
        xb_ref[...] = x_ref[...].astype(BF16)
        o_ref[...] = jnp.zeros_like(o_ref)

    def chunk(r0, width):
        xb = xb_ref[r0:r0 + sub, :]
        gate = jnp.dot(xb, wg_ref[:, :width], preferred_element_type=F32)
        up = jnp.dot(xb, wu_ref[:, :width], preferred_element_type=F32)
        h = (gate * jax.nn.sigmoid(gate) * up).astype(BF16)
        return jnp.dot(h, wd_ref[:width, :], preferred_element_type=F32)

    @pl.when(f < nf - 1)
    def _():
        for r0 in range(0, tm, sub):
            o_ref[r0:r0 + sub, :] += chunk(r0, wg_ref.shape[1])

    @pl.when(f == nf - 1)
    def _():
        for r0 in range(0, tm, sub):
            z = alpha * x_ref[r0:r0 + sub, :] + (o_ref[r0:r0 + sub, :] + chunk(r0, tail))
            o_ref[r0:r0 + sub, :] = _layer_norm_rows(z, g_ref[...], b_ref[...])


def _ffn_ln(x, wg, wu, wd, g, b, layer, k, *, alpha, tm, tf):
    m, d = x.shape
    n_hidden = wg.shape[-1]
    nf = pl.cdiv(n_hidden, tf)
    tail = n_hidden - (nf - 1) * tf
    assert nf >= 2 and tail % LANES == 0
    return pl.pallas_call(
        functools.partial(_ffn_ln_kernel, nf=nf, tail=tail, alpha=alpha),
        grid=(m // tm, nf),
        in_specs=[
            pl.BlockSpec((tm, d), lambda i, f: (i, 0)),
            pl.BlockSpec((None, None, d, tf), lambda i, f: (layer, k, 0, f)),
            pl.BlockSpec((None, None, d, tf), lambda i, f: (layer, k, 0, f)),
            pl.BlockSpec((None, None, tf, d), lambda i, f: (layer, k, f, 0)),
            pl.BlockSpec((None, None, 1, d), lambda i, f: (layer, 2 * k, 0, 0)),
            pl.BlockSpec((None, None, 1, d), lambda i, f: (layer, 2 * k, 0, 0)),
        ],
        out_specs=pl.BlockSpec((tm, d), lambda i, f: (i, 0)),
        out_shape=jax.ShapeDtypeStruct((m, d), F32),
        scratch_shapes=[pltpu.VMEM((tm, d), BF16)],
        compiler_params=_compiler_params(2),
        name="ffn_ln",
    )(x, wg, wu, wd, g, b)


def _s5_kernel(x_ref, rp_ref, rpt_ref, p_ref, w1_ref, w2_ref, a1_ref, a2_ref, y_ref, *, n_chunks):
    sub = rp_ref.shape[0]
    cps = sub // S5_CHUNK
    subs_per_seq = n_chunks // cps
    n_seq = x_ref.shape[0] // (n_chunks * S5_CHUNK)
    groups = w1_ref.shape[0]
    width = w1_ref.shape[1]
    n_state = (w1_ref.shape[2] - width) // 2
    half = n_state // 2
    n_scan = n_chunks.bit_length() - 1
    j = lax.broadcasted_iota(jnp.int32, (n_chunks, n_state), 0)

    def cmul(a1, a2, v):
        return a1 * v + a2 * pltpu.roll(v, half, axis=1)

    def project(q):
        stacked = []
        for sb in range(q * subs_per_seq, (q + 1) * subs_per_seq):
            xs = x_ref[sb * sub:(sb + 1) * sub, :].astype(BF16)
            stacked.append(jnp.dot(rp_ref[...], xs, preferred_element_type=F32).astype(BF16))
        ucat = jnp.concatenate(
            [jnp.concatenate([st[s * cps:(s + 1) * cps] for st in stacked], axis=0) for s in range(S5_CHUNK)],
            axis=1)
        uperm = jnp.dot(ucat, p_ref[...], preferred_element_type=F32).astype(BF16)
        return [jnp.dot(uperm[:, gl * width:(gl + 1) * width], w1_ref[gl], preferred_element_type=F32)
                for gl in range(groups)]

    def scan(r1s):
        s_ins = []
        for gl, r1 in enumerate(r1s):
            s_fwd = r1[:, width:width + n_state]
            s_bwd = r1[:, width + n_state:]
            xf = jnp.where(j >= 1, pltpu.roll(s_fwd, 1, axis=0), 0.0)
            xb = jnp.where(j < n_chunks - 1, pltpu.roll(s_bwd, n_chunks - 1, axis=0), 0.0)
            for m in range(n_scan):
                sh = 1 << m
                sf = jnp.where(j >= sh, pltpu.roll(xf, sh, axis=0), 0.0)
                xf = xf + cmul(a1_ref[gl, 0, m:m + 1, :], a2_ref[gl, 0, m:m + 1, :], sf)
                sb = jnp.where(j < n_chunks - sh, pltpu.roll(xb, n_chunks - sh, axis=0), 0.0)
                xb = xb + cmul(a1_ref[gl, 1, m:m + 1, :], a2_ref[gl, 1, m:m + 1, :], sb)
            s_ins.append(jnp.concatenate([xf, xb], axis=1).astype(BF16))
        return s_ins

    def readout(q, r1s, s_ins):
        ys = [(r1[:, :width] + lax.dot_general(s_in, w2_ref[gl], NT_DIMS, preferred_element_type=F32)).astype(BF16)
              for gl, (r1, s_in) in enumerate(zip(r1s, s_ins))]
        yperm = lax.dot_general(jnp.concatenate(ys, axis=1), p_ref[...], NT_DIMS, preferred_element_type=F32)
        for k in range(subs_per_seq):
            ystack = jnp.concatenate([yperm[k * cps:(k + 1) * cps, s * LANES:(s + 1) * LANES].astype(BF16)
                                      for s in range(S5_CHUNK)], axis=0)
            sb = q * subs_per_seq + k
            y_ref[sb * sub:(sb + 1) * sub, :] = jnp.dot(rpt_ref[...], ystack, preferred_element_type=F32)

    r1s = [project(q) for q in range(n_seq)]
    for q in range(n_seq):
        readout(q, r1s[q], scan(r1s[q]))


def _s5_ssm(x, row_perm, lane_perm, w1, w2, a1, a2, *, n_chunks, tokens):
    m, d = x.shape
    gpt = LANES // S5_GROUP
    const2 = lambda t, r: (0, 0)
    return pl.pallas_call(
        functools.partial(_s5_kernel, n_chunks=n_chunks),
        grid=(d // LANES, m // tokens),
        in_specs=[
            pl.BlockSpec((tokens, LANES), lambda t, r: (r, t)),
            pl.BlockSpec(row_perm.shape, const2),
            pl.BlockSpec(row_perm.shape, const2),
            pl.BlockSpec(lane_perm.shape, const2),
            pl.BlockSpec((gpt,) + w1.shape[1:], lambda t, r: (t, 0, 0)),
            pl.BlockSpec((gpt,) + w2.shape[1:], lambda t, r: (t, 0, 0)),
            pl.BlockSpec((gpt,) + a1.shape[1:], lambda t, r: (t, 0, 0, 0)),
            pl.BlockSpec((gpt,) + a2.shape[1:], lambda t, r: (t, 0, 0, 0)),
        ],
        out_specs=pl.BlockSpec((tokens, LANES), lambda t, r: (r, t)),
        out_shape=jax.ShapeDtypeStruct((m, d), F32),
        compiler_params=_compiler_params(2),
        name="s5_ssm",
    )(x, row_perm, row_perm.T, lane_perm, w1, w2, a1, a2)


def _s5_row_shuffle(sub):
    cps = sub // S5_CHUNK
    dst = np.arange(sub)
    src = (dst % cps) * S5_CHUNK + dst // cps
    col = lax.broadcasted_iota(jnp.int32, (sub, sub), 1)
    return (col == jnp.asarray(src, jnp.int32)[:, None]).astype(BF16)


def _s5_lane_shuffle():
    gpt = LANES // S5_GROUP
    n = S5_CHUNK * LANES
    src = np.arange(n)
    step, grp, ch = src // LANES, (src % LANES) // S5_GROUP, src % S5_GROUP
    dst = grp * (S5_CHUNK * S5_GROUP) + step * S5_GROUP + ch
    assert gpt * S5_CHUNK * S5_GROUP == n
    col = lax.broadcasted_iota(jnp.int32, (n, n), 1)
    return (col == jnp.asarray(dst, jnp.int32)[:, None]).astype(BF16)


def _s5_tables(lam_re, lam_im, log_dt, b_re, b_im, c_re, c_im, n_chunks):
    t = S5_CHUNK
    n_groups, p = lam_re.shape[1:]
    gc = b_re.shape[-1]
    lr = jnp.minimum(lam_re.astype(F32), MIN_NEG_RE)
    li = lam_im.astype(F32)
    dt = jnp.exp(log_dt.astype(F32))[..., None]
    x, th = lr * dt, li * dt
    mag = jnp.exp(x)
    lb_re, lb_im = mag * jnp.cos(th), mag * jnp.sin(th)
    den = lr * lr + li * li
    nr, ni = lb_re - 1.0, lb_im
    f_re = (nr * lr + ni * li) / den
    f_im = (ni * lr - nr * li) / den
    brt = jnp.swapaxes(b_re.astype(F32), -1, -2)
    bit = jnp.swapaxes(b_im.astype(F32), -1, -2)
    bb_re = f_re[:, :, None] * brt - f_im[:, :, None] * bit
    bb_im = f_re[:, :, None] * bit + f_im[:, :, None] * brt
    cr, ci = c_re.astype(F32), c_im.astype(F32)

    kk = jnp.arange(t + 1, dtype=F32)[None, None, :, None, None]
    pmag = jnp.exp(kk * x[:, :, None, None, :])
    pw_re = pmag * jnp.cos(kk * th[:, :, None, None, :])
    pw_im = pmag * jnp.sin(kk * th[:, :, None, None, :])

    cp = jnp.concatenate([cr[:, :, None] * pw_re - ci[:, :, None] * pw_im,
                          -(cr[:, :, None] * pw_im + ci[:, :, None] * pw_re)], axis=-1)
    pb = jnp.concatenate([pw_re * bb_re[:, :, None] - pw_im * bb_im[:, :, None],
                          pw_re * bb_im[:, :, None] + pw_im * bb_re[:, :, None]], axis=-1)
    kern = jnp.einsum('dgkcp,dgep->dgkce', cp[:, :, :t], pb[:, :, 0], precision=HIGHEST)
    kern = jnp.concatenate([kern[0], kern[1]], axis=1)

    sig = np.arange(t)[:, None]
    tau = np.arange(t)[None, :]
    sel = np.zeros((2 * t, t, t), np.float32)
    for k in range(t):
        sel[k] = (tau - sig == k)
        sel[t + k] = (sig - tau == k)
    toep = jnp.einsum('kst,gkce->gsetc', jnp.asarray(sel), kern, precision=HIGHEST)
    toep = toep.reshape(n_groups, t * gc, t * gc)

    sin_f = pb[0][:, t - 1::-1][:, :t].reshape(n_groups, t * gc, 2 * p)
    sin_b = pb[1][:, :t].reshape(n_groups, t * gc, 2 * p)
    w1 = jnp.concatenate([toep, sin_f, sin_b], axis=2)
    out_f = cp[0][:, 1:t + 1].reshape(n_groups, t * gc, 2 * p)
    out_b = cp[1][:, t:0:-1].reshape(n_groups, t * gc, 2 * p)
    w2 = jnp.concatenate([out_f, out_b], axis=2)

    a_re, a_im = pw_re[:, :, t, 0], pw_im[:, :, t, 0]
    a1, a2 = [], []
    for _ in range(n_chunks.bit_length() - 1):
        a1.append(jnp.concatenate([a_re, a_re], axis=-1))
        a2.append(jnp.concatenate([-a_im, a_im], axis=-1))
        a_re, a_im = a_re * a_re - a_im * a_im, 2.0 * a_re * a_im
    a1 = jnp.transpose(jnp.stack(a1, axis=2), (1, 0, 2, 3))
    a2 = jnp.transpose(jnp.stack(a2, axis=2), (1, 0, 2, 3))
    return w1.astype(BF16), w2.astype(BF16), a1, a2


def _glu_ln_kernel(y_ref, x_ref, d_ref, wv_ref, wg_ref, g_ref, b_ref, o_ref, *, alpha):
    tm = o_ref.shape[0]
    sub = math.gcd(tm, EPILOGUE_SUB_ROWS)
    for r0 in range(0, tm, sub):
        x = x_ref[r0:r0 + sub, :]
        a = jax.nn.gelu(y_ref[r0:r0 + sub, :] + d_ref[...] * x).astype(BF16)
        val = jnp.dot(a, wv_ref[...], preferred_element_type=F32)
        gate = jnp.dot(a, wg_ref[...], preferred_element_type=F32)
        z = alpha * x + val * jax.nn.sigmoid(gate)
        o_ref[r0:r0 + sub, :] = _layer_norm_rows(z, g_ref[...], b_ref[...])


def _glu_ln(y, x, d_skip, wv, wg, g, b, *, alpha, tm):
    m, d = x.shape
    rows = pl.BlockSpec((tm, d), lambda i: (i, 0))
    whole = lambda a: pl.BlockSpec(a.shape, lambda i: (0,) * a.ndim)
    return pl.pallas_call(
        functools.partial(_glu_ln_kernel, alpha=alpha),
        grid=(m // tm,),
        in_specs=[rows, rows, whole(d_skip), whole(wv), whole(wg), whole(g), whole(b)],
        out_specs=rows,
        out_shape=jax.ShapeDtypeStruct((m, d), F32),
        compiler_params=_compiler_params(1),
        name="glu_ln",
    )(y, x, d_skip, wv, wg, g, b)


def _qkv_kernel(x_ref, w_ref, o_ref, xb_ref, *, n_q_tiles, scale):
    n = pl.program_id(1)

    @pl.when(n == 0)
    def _():
        xb_ref[...] = x_ref[...].astype(BF16)

    r = jnp.dot(xb_ref[...], w_ref[...], preferred_element_type=F32)
    r = r * jnp.where(n < n_q_tiles, scale, 1.0)
    o_ref[...] = r.astype(BF16)


def _qkv_proj(x, w, *, d_q, scale, tm, tn):
    m, d = x.shape
    n_out = w.shape[1]
    return pl.pallas_call(
        functools.partial(_qkv_kernel, n_q_tiles=d_q // tn, scale=scale),
        grid=(m // tm, n_out // tn),
        in_specs=[
            pl.BlockSpec((tm, d), lambda i, n: (i, 0)),
            pl.BlockSpec((d, tn), lambda i, n: (0, n)),
        ],
        out_specs=pl.BlockSpec((tm, tn), lambda i, n: (i, n)),
        out_shape=jax.ShapeDtypeStruct((m, n_out), BF16),
        scratch_shapes=[pltpu.VMEM((tm, d), BF16)],
        compiler_params=_compiler_params(2),
        name="qkv_proj",
    )(x, w)


NATTEN_ROWS_PER_ITER = 8


def _natten_kernel(q_ref, k_ref, v_ref, bias_ref, o_ref, *, grid_rows, kh, rows_per_iter):
    w = GRID_W

    def row_block(rb, carry):
        pending = []
        for u in range(rows_per_iter):
            r = rb * rows_per_iter + u
            rs = jnp.clip(r - kh // 2, 0, grid_rows - kh)
            q = q_ref[0, pl.ds(pl.multiple_of(r * w, w), w), :]
            k = k_ref[0, pl.ds(pl.multiple_of(rs * w, w), kh * w), :]
            s = lax.dot_general(q, k, NT_DIMS, preferred_element_type=F32)
            pending.append((r, rs, s + bias_ref[0, rs - r + (WIN_H - 1)]))
        for r, rs, s in pending:
            v = v_ref[0, pl.ds(pl.multiple_of(rs * w, w), kh * w), :]
            p = jnp.exp(s - jnp.max(s, axis=-1, keepdims=True))
            denom = jnp.sum(p, axis=-1, keepdims=True)
            o = jnp.dot(p.astype(BF16), v, preferred_element_type=F32) / denom
            o_ref[0, pl.ds(pl.multiple_of(r * w, w), w), :] = o.astype(BF16)
        return carry

    lax.fori_loop(0, grid_rows // rows_per_iter, row_block, 0)


def _natten(qkv, bias, *, kh):
    bsz, seq, three_d = qkv.shape
    n_heads = bias.shape[0]
    dh = three_d // (3 * n_heads)
    grid_rows = seq // GRID_W
    rows_per_iter = math.gcd(grid_rows, NATTEN_ROWS_PER_ITER)
    return pl.pallas_call(
        functools.partial(_natten_kernel, grid_rows=grid_rows, kh=kh, rows_per_iter=rows_per_iter),
        grid=(n_heads, bsz),
        in_specs=[
            pl.BlockSpec((1, seq, dh), lambda h, b: (b, 0, h)),
            pl.BlockSpec((1, seq, dh), lambda h, b: (b, 0, n_heads + h)),
            pl.BlockSpec((1, seq, dh), lambda h, b: (b, 0, 2 * n_heads + h)),
            pl.BlockSpec((1,) + bias.shape[1:], lambda h, b: (h, 0, 0, 0)),
        ],
        out_specs=pl.BlockSpec((1, seq, dh), lambda h, b: (b, 0, h)),
        out_shape=jax.ShapeDtypeStruct((bsz, seq, n_heads * dh), BF16),
        compiler_params=_compiler_params(2),
        name="natten",
    )(qkv, qkv, qkv, bias)


def _natten_bias(rpb, kh):
    cols = np.arange(GRID_W)
    col_start = np.clip(cols - WIN_W // 2, 0, GRID_W - WIN_W)
    kc = np.arange(GRID_W)
    in_win = (kc[None, :] >= col_start[:, None]) & (kc[None, :] < col_start[:, None] + WIN_W)
    col_off = kc[None, :] - cols[:, None] + (WIN_W - 1)
    onehot = ((col_off[None] == np.arange(2 * WIN_W - 1)[:, None, None]) & in_win[None]).astype(np.float32)
    n_off = 2 * WIN_H - kh
    r2 = jnp.stack([rpb.astype(F32)[:, o:o + kh] for o in range(n_off)], axis=1)
    b = jnp.einsum('hriw,wck->hrcik', r2, jnp.asarray(onehot), precision=HIGHEST)
    b = b + jnp.asarray(np.where(in_win, 0.0, MASK_VALUE), F32)[None, None, :, None, :]
    return b.reshape(b.shape[0], n_off, GRID_W, kh * GRID_W)


def _proj_ln_kernel(a_ref, x_ref, w_ref, g_ref, b_ref, o_ref, *, alpha):
    tm = o_ref.shape[0]
    sub = math.gcd(tm, EPILOGUE_SUB_ROWS)
    for r0 in range(0, tm, sub):
        z = alpha * x_ref[r0:r0 + sub, :] + jnp.dot(a_ref[r0:r0 + sub, :], w_ref[...], preferred_element_type=F32)
        o_ref[r0:r0 + sub, :] = _layer_norm_rows(z, g_ref[...], b_ref[...])


def _proj_ln(a, x, w, g, b, *, alpha, tm):
    m, d = x.shape
    whole = lambda v: pl.BlockSpec(v.shape, lambda i: (0,) * v.ndim)
    return pl.pallas_call(
        functools.partial(_proj_ln_kernel, alpha=alpha),
        grid=(m // tm,),
        in_specs=[pl.BlockSpec((tm, a.shape[1]), lambda i: (i, 0)), pl.BlockSpec((tm, d), lambda i: (i, 0)),
                  whole(w), whole(g), whole(b)],
        out_specs=pl.BlockSpec((tm, d), lambda i: (i, 0)),
        out_shape=jax.ShapeDtypeStruct((m, d), F32),
        compiler_params=_compiler_params(1),
        name="proj_ln",
    )(a, x, w, g, b)


def _tiles(m, d, bsz, seq):
    def rows(pref):
        return pref if m % pref == 0 else m
    s5_seqs = max(1, min(bsz, S5_TOKENS_PER_STEP // seq))
    while bsz % s5_seqs:
        s5_seqs -= 1
    return dict(tm=rows(512), ffn_tm=rows(1024), ffn_tf=512, qkv_tm=rows(1024), qkv_tn=d,
                s5_tokens=s5_seqs * seq, s5_sub=math.gcd(seq, S5_SUB_TOKENS))


def _cast_kernel(w_ref, o_ref, *, scale):
    w = w_ref[...]
    o_ref[...] = (w if scale is None else w * scale).astype(BF16)


def _cast_bf16(w, *, tr, scale=None):
    a, b, r, c = w.shape
    assert r % tr == 0
    spec = pl.BlockSpec((None, None, tr, c), lambda i, j, k: (i, j, k, 0))
    return pl.pallas_call(
        functools.partial(_cast_kernel, scale=scale),
        grid=(a, b, r // tr),
        in_specs=[spec],
        out_specs=spec,
        out_shape=jax.ShapeDtypeStruct(w.shape, BF16),
        compiler_params=_compiler_params(3),
        name="cast_bf16",
    )(w)


def _row_block(r, target):
    return max(t for t in range(16, target + 1, 16) if r % t == 0)


def kernel(x, ffn_w_gate, ffn_w_up, ffn_w_down, ln_g, ln_b, s5_lam_re, s5_lam_im, s5_log_dt, s5_b_re, s5_b_im, s5_c_re, s5_c_im, s5_d, s5_w_glu_val, s5_w_glu_gate, na_w_qkv, na_rpb, na_w_out):
    bsz, seq, d = x.shape
    depth = ffn_w_gate.shape[0]
    alpha = (2 * depth) ** 0.25
    m = bsz * seq
    n_chunks = seq // S5_CHUNK
    t = _tiles(m, d, bsz, seq)
    tm = t["tm"]
    grid_rows = seq // GRID_W
    kh = min(WIN_H, grid_rows)
    dh = d // N_HEADS

    wg_all = _cast_bf16(ffn_w_gate, tr=_row_block(d, 256))
    wu_all = _cast_bf16(ffn_w_up, tr=_row_block(d, 256))
    wd_all = _cast_bf16(ffn_w_down, tr=_row_block(ffn_w_down.shape[2], 768), scale=0.5)
    g_all = ln_g[:, :, None, :]
    b_all = ln_b[:, :, None, :]

    def ffn(h, i, k):
        return _ffn_ln(h, wg_all, wu_all, wd_all, g_all, b_all, i, k, alpha=alpha, tm=t["ffn_tm"], tf=t["ffn_tf"])

    h = x.reshape(m, d)
    for i in range(depth):
        h = ffn(h, i, 0)
        jm = i // N_MIXERS
        g1, b1 = ln_g[i, 1][None], ln_b[i, 1][None]
        if i % N_MIXERS == 0:
            w1, w2, a1, a2 = _s5_tables(s5_lam_re[jm], s5_lam_im[jm], s5_log_dt[jm], s5_b_re[jm], s5_b_im[jm],
                                        s5_c_re[jm], s5_c_im[jm], n_chunks)
            y = _s5_ssm(h, _s5_row_shuffle(t["s5_sub"]), _s5_lane_shuffle(), w1, w2, a1, a2, n_chunks=n_chunks,
                        tokens=t["s5_tokens"])
            h = _glu_ln(y, h, s5_d[jm][None], s5_w_glu_val[jm].astype(BF16), s5_w_glu_gate[jm].astype(BF16),
                        g1, b1, alpha=alpha, tm=tm)
        else:
            qkv = _qkv_proj(h, na_w_qkv[jm].astype(BF16), d_q=d, scale=dh ** -0.5, tm=t["qkv_tm"], tn=t["qkv_tn"])
            att = _natten(qkv.reshape(bsz, seq, 3 * d), _natten_bias(na_rpb[jm], kh), kh=kh)
            h = _proj_ln(att.reshape(m, d), h, na_w_out[jm].astype(BF16), g1, b1, alpha=alpha, tm=tm)
        h = ffn(h, i, 1)
    return h.reshape(bsz, seq, d)
```

```python
import functools
import math

import jax
import jax.numpy as jnp
import numpy as np
from jax import lax
from jax.experimental import pallas as pl
from jax.experimental.pallas import tpu as pltpu

GRID_W = 64
N_HEADS = 16
WIN_H = 8
WIN_W = 16
S5_GROUP = 16
N_MIXERS = 2
LN_EPS = 1e-5
MIN_NEG_RE = -1e-4

LANES = 128
VMEM_BYTES_V7X = 64 * 1024 * 1024
VMEM_LIMIT_BYTES = VMEM_BYTES_V7X - 2 * 1024 * 1024

FFN_SUB_ROWS = 512
EPILOGUE_SUB_ROWS = 256
S5_CHUNK = 16
S5_SUB_TOKENS = 512
S5_TOKENS_PER_STEP = 8192
MASK_VALUE = -1e30

F32 = jnp.float32
BF16 = jnp.bfloat16
HIGHEST = lax.Precision.HIGHEST
NT_DIMS = (((1,), (1,)), ((), ()))


def _compiler_params(n_axes):
    return pltpu.CompilerParams(dimension_semantics=("arbitrary",) * n_axes,
                                vmem_limit_bytes=VMEM_LIMIT_BYTES)


def _layer_norm_rows(z, g, b):
    mu = jnp.mean(z, axis=-1, keepdims=True)
    zc = z - mu
    var = jnp.mean(zc * zc, axis=-1, keepdims=True)
    return zc * lax.rsqrt(var + LN_EPS) * g + b


def _ffn_ln_kernel(x_ref, wg_ref, wu_ref, wd_ref, g_ref, b_ref, o_ref, xb_ref, *, nf, tail, alpha):
    f = pl.program_id(1)
    tm = o_ref.shape[0]
    sub = math.gcd(tm, FFN_SUB_ROWS)

    def chunk(xb, width):
        gate = jnp.dot(xb, wg_ref[:, :width], preferred_element_type=F32)
        up = jnp.dot(xb, wu_ref[:, :width], preferred_element_type=F32)
        h = (gate * jax.nn.sigmoid(gate) * up).astype(BF16)
        return jnp.dot(h, wd_ref[:width, :], preferred_element_type=F32)

    @pl.when(f == 0)
    def _():
        for r0 in range(0, tm, sub):
            xb = x_ref[r0:r0 + sub, :].astype(BF16)
            xb_ref[r0:r0 + sub, :] = xb
            o_ref[r0:r0 + sub, :] = chunk(xb, wg_ref.shape[1])

    @pl.when(jnp.logical_and(f > 0, f < nf - 1))
    def _():
        for r0 in range(0, tm, sub):
            o_ref[r0:r0 + sub, :] += chunk(xb_ref[r0:r0 + sub, :], wg_ref.shape[1])

    @pl.when(f == nf - 1)
    def _():
        for r0 in range(0, tm, sub):
            z = alpha * x_ref[r0:r0 + sub, :] + (o_ref[r0:r0 + sub, :] + chunk(xb_ref[r0:r0 + sub, :], tail))
            o_ref[r0:r0 + sub, :] = _layer_norm_rows(z, g_ref[...], b_ref[...])


def _ffn_ln(x, wg, wu, wd, g, b, layer, k, *, alpha, tm, tf):
    m, d = x.shape
    n_hidden = wg.shape[-1]
    nf = pl.cdiv(n_hidden, tf)
    tail = n_hidden - (nf - 1) * tf
    assert nf >= 2 and tail % LANES == 0
    return pl.pallas_call(
        functools.partial(_ffn_ln_kernel, nf=nf, tail=tail, alpha=alpha),
        grid=(m // tm, nf),
        in_specs=[
            pl.BlockSpec((tm, d), lambda i, f: (i, 0)),
            pl.BlockSpec((None, None, d, tf), lambda i, f: (layer, k, 0, f)),
            pl.BlockSpec((None, None, d, tf), lambda i, f: (layer, k, 0, f)),
            pl.BlockSpec((None, None, tf, d), lambda i, f: (layer, k, f, 0)),
            pl.BlockSpec((None, None, 1, d), lambda i, f: (layer, 2 * k, 0, 0)),
            pl.BlockSpec((None, None, 1, d), lambda i, f: (layer, 2 * k, 0, 0)),
        ],
        out_specs=pl.BlockSpec((tm, d), lambda i, f: (i, 0)),
        out_shape=jax.ShapeDtypeStruct((m, d), F32),
        scratch_shapes=[pltpu.VMEM((tm, d), BF16)],
        compiler_params=_compiler_params(2),
        name="ffn_ln",
    )(x, wg, wu, wd, g, b)


def _s5_kernel(x_ref, rp_ref, rpt_ref, p_ref, toep_ref, win_ref, w2_ref, a1_ref, a2_ref, y_ref, *, n_chunks):
    sub = rp_ref.shape[0]
    cps = sub // S5_CHUNK
    subs_per_seq = n_chunks // cps
    n_seq = x_ref.shape[0] // (n_chunks * S5_CHUNK)
    groups = toep_ref.shape[0]
    width = toep_ref.shape[1]
    n_state = win_ref.shape[2] // 2
    half = n_state // 2
    n_scan = n_chunks.bit_length() - 1
    j = lax.broadcasted_iota(jnp.int32, (n_chunks, n_state), 0)

    def cmul(a1, a2, v):
        return a1 * v + a2 * pltpu.roll(v, half, axis=1)

    def project(q):
        stacked = []
        for sb in range(q * subs_per_seq, (q + 1) * subs_per_seq):
            xs = x_ref[sb * sub:(sb + 1) * sub, :].astype(BF16)
            stacked.append(jnp.dot(rp_ref[...], xs, preferred_element_type=F32).astype(BF16))
        ucat = jnp.concatenate(
            [jnp.concatenate([st[s * cps:(s + 1) * cps] for st in stacked], axis=0) for s in range(S5_CHUNK)],
            axis=1)
        uperm = jnp.dot(ucat, p_ref[...], preferred_element_type=F32).astype(BF16)
        us = [uperm[:, gl * width:(gl + 1) * width] for gl in range(groups)]
        return [(jnp.dot(u, toep_ref[gl], preferred_element_type=F32),
                 jnp.dot(u, win_ref[gl], preferred_element_type=F32)) for gl, u in enumerate(us)]

    def scan(r1s):
        s_ins = []
        for gl, (_, st) in enumerate(r1s):
            s_fwd = st[:, :n_state]
            s_bwd = st[:, n_state:]
            xf = jnp.where(j >= 1, pltpu.roll(s_fwd, 1, axis=0), 0.0)
            xb = jnp.where(j < n_chunks - 1, pltpu.roll(s_bwd, n_chunks - 1, axis=0), 0.0)
            for m in range(n_scan):
                sh = 1 << m
                sf = jnp.where(j >= sh, pltpu.roll(xf, sh, axis=0), 0.0)
                xf = xf + cmul(a1_ref[gl, 0, m:m + 1, :], a2_ref[gl, 0, m:m + 1, :], sf)
                sb = jnp.where(j < n_chunks - sh, pltpu.roll(xb, n_chunks - sh, axis=0), 0.0)
                xb = xb + cmul(a1_ref[gl, 1, m:m + 1, :], a2_ref[gl, 1, m:m + 1, :], sb)
            s_ins.append(jnp.concatenate([xf, xb], axis=1).astype(BF16))
        return s_ins

    def readout(q, r1s, s_ins):
        ys = [(y_in + lax.dot_general(s_in, w2_ref[gl], NT_DIMS, preferred_element_type=F32)).astype(BF16)
              for gl, ((y_in, _), s_in) in enumerate(zip(r1s, s_ins))]
        yperm = lax.dot_general(jnp.concatenate(ys, axis=1), p_ref[...], NT_DIMS, preferred_element_type=F32)
        for k in range(subs_per_seq):
            ystack = jnp.concatenate([yperm[k * cps:(k + 1) * cps, s * LANES:(s + 1) * LANES].astype(BF16)
                                      for s in range(S5_CHUNK)], axis=0)
            sb = q * subs_per_seq + k
            y_ref[sb * sub:(sb + 1) * sub, :] = jnp.dot(rpt_ref[...], ystack, preferred_element_type=F32)

    r1s = [project(q) for q in range(n_seq)]
    for q in range(n_seq):
        readout(q, r1s[q], scan(r1s[q]))


def _s5_ssm(x, row_perm, lane_perm, toep, w_in, w2, a1, a2, *, n_chunks, tokens):
    m, d = x.shape
    gpt = LANES // S5_GROUP
    const2 = lambda t, r: (0, 0)
    return pl.pallas_call(
        functools.partial(_s5_kernel, n_chunks=n_chunks),
        grid=(d // LANES, m // tokens),
        in_specs=[
            pl.BlockSpec((tokens, LANES), lambda t, r: (r, t)),
            pl.BlockSpec(row_perm.shape, const2),
            pl.BlockSpec(row_perm.shape, const2),
            pl.BlockSpec(lane_perm.shape, const2),
            pl.BlockSpec((gpt,) + toep.shape[1:], lambda t, r: (t, 0, 0)),
            pl.BlockSpec((gpt,) + w_in.shape[1:], lambda t, r: (t, 0, 0)),
            pl.BlockSpec((gpt,) + w2.shape[1:], lambda t, r: (t, 0, 0)),
            pl.BlockSpec((gpt,) + a1.shape[1:], lambda t, r: (t, 0, 0, 0)),
            pl.BlockSpec((gpt,) + a2.shape[1:], lambda t, r: (t, 0, 0, 0)),
        ],
        out_specs=pl.BlockSpec((tokens, LANES), lambda t, r: (r, t)),
        out_shape=jax.ShapeDtypeStruct((m, d), F32),
        compiler_params=_compiler_params(2),
        name="s5_ssm",
    )(x, row_perm, row_perm.T, lane_perm, toep, w_in, w2, a1, a2)


def _s5_row_shuffle(sub):
    cps = sub // S5_CHUNK
    dst = np.arange(sub)
    src = (dst % cps) * S5_CHUNK + dst // cps
    col = lax.broadcasted_iota(jnp.int32, (sub, sub), 1)
    return (col == jnp.asarray(src, jnp.int32)[:, None]).astype(BF16)


def _s5_lane_shuffle():
    gpt = LANES // S5_GROUP
    n = S5_CHUNK * LANES
    src = np.arange(n)
    step, grp, ch = src // LANES, (src % LANES) // S5_GROUP, src % S5_GROUP
    dst = grp * (S5_CHUNK * S5_GROUP) + step * S5_GROUP + ch
    assert gpt * S5_CHUNK * S5_GROUP == n
    col = lax.broadcasted_iota(jnp.int32, (n, n), 1)
    return (col == jnp.asarray(dst, jnp.int32)[:, None]).astype(BF16)


def _s5_tables(lam_re, lam_im, log_dt, b_re, b_im, c_re, c_im, n_chunks):
    t = S5_CHUNK
    n_groups, p = lam_re.shape[1:]
    gc = b_re.shape[-1]
    lr = jnp.minimum(lam_re.astype(F32), MIN_NEG_RE)
    li = lam_im.astype(F32)
    dt = jnp.exp(log_dt.astype(F32))[..., None]
    x, th = lr * dt, li * dt
    mag = jnp.exp(x)
    lb_re, lb_im = mag * jnp.cos(th), mag * jnp.sin(th)
    den = lr * lr + li * li
    nr, ni = lb_re - 1.0, lb_im
    f_re = (nr * lr + ni * li) / den
    f_im = (ni * lr - nr * li) / den
    brt = jnp.swapaxes(b_re.astype(F32), -1, -2)
    bit = jnp.swapaxes(b_im.astype(F32), -1, -2)
    bb_re = f_re[:, :, None] * brt - f_im[:, :, None] * bit
    bb_im = f_re[:, :, None] * bit + f_im[:, :, None] * brt
    cr, ci = c_re.astype(F32), c_im.astype(F32)

    kk = jnp.arange(t + 1, dtype=F32)[None, None, :, None]
    pmag = jnp.exp(kk * x[:, :, None, :])
    pw_re = pmag * jnp.cos(kk * th[:, :, None, :])
    pw_im = pmag * jnp.sin(kk * th[:, :, None, :])
    pw_re, pw_im = lax.optimization_barrier((pw_re, pw_im))

    def powers(idx):
        re = jnp.stack([pw_re[dr][:, idx[dr]] for dr in range(2)])
        im = jnp.stack([pw_im[dr][:, idx[dr]] for dr in range(2)])
        return re[:, :, :, None, :], im[:, :, :, None, :]

    def times_c(idx):
        pr, pi = powers(idx)
        return jnp.concatenate([cr[:, :, None] * pr - ci[:, :, None] * pi,
                                -(cr[:, :, None] * pi + ci[:, :, None] * pr)], axis=-1)

    def times_b(idx):
        pr, pi = powers(idx)
        return jnp.concatenate([pr * bb_re[:, :, None] - pi * bb_im[:, :, None],
                                pr * bb_im[:, :, None] + pi * bb_re[:, :, None]], axis=-1)

    def per_group(v):
        return jnp.transpose(v, (1, 2, 3, 0, 4)).reshape(n_groups, t * gc, 4 * p).astype(BF16)

    steps = np.arange(t)
    w_in = per_group(times_b(np.stack([t - 1 - steps, steps])))
    w2 = per_group(times_c(np.stack([steps + 1, t - steps])))

    kern = jnp.einsum('dgkcp,dgep->dgkce', times_c(np.stack([steps, steps])),
                      jnp.concatenate([bb_re, bb_im], axis=-1), precision=HIGHEST)
    kern = jnp.concatenate([kern[0], kern[1]], axis=1)
    sig = steps[:, None]
    tau = steps[None, :]
    sel = np.zeros((2 * t, t, t), np.float32)
    for k in range(t):
        sel[k] = (tau - sig == k)
        sel[t + k] = (sig - tau == k)
    toep = jnp.einsum('kst,gkce->gsetc', jnp.asarray(sel), kern, precision=HIGHEST)
    toep = toep.reshape(n_groups, t * gc, t * gc).astype(BF16)

    a_re, a_im = pw_re[:, :, t], pw_im[:, :, t]
    a1, a2 = [], []
    for _ in range(n_chunks.bit_length() - 1):
        a1.append(jnp.concatenate([a_re, a_re], axis=-1))
        a2.append(jnp.concatenate([-a_im, a_im], axis=-1))
        a_re, a_im = a_re * a_re - a_im * a_im, 2.0 * a_re * a_im
    a1 = jnp.transpose(jnp.stack(a1, axis=2), (1, 0, 2, 3))
    a2 = jnp.transpose(jnp.stack(a2, axis=2), (1, 0, 2, 3))
    return toep, w_in, w2, a1, a2


def _glu_ln_kernel(y_ref, x_ref, d_ref, wv_ref, wg_ref, g_ref, b_ref, o_ref, *, alpha):
    tm = o_ref.shape[0]
    sub = math.gcd(tm, EPILOGUE_SUB_ROWS)
    for r0 in range(0, tm, sub):
        x = x_ref[r0:r0 + sub, :]
        a = jax.nn.gelu(y_ref[r0:r0 + sub, :] + d_ref[...] * x).astype(BF16)
        val = jnp.dot(a, wv_ref[...], preferred_element_type=F32)
        gate = jnp.dot(a, wg_ref[...], preferred_element_type=F32)
        z = alpha * x + val * jax.nn.sigmoid(gate)
        o_ref[r0:r0 + sub, :] = _layer_norm_rows(z, g_ref[...], b_ref[...])


def _glu_ln(y, x, d_skip, wv, wg, g, b, *, alpha, tm):
    m, d = x.shape
    rows = pl.BlockSpec((tm, d), lambda i: (i, 0))
    whole = lambda a: pl.BlockSpec(a.shape, lambda i: (0,) * a.ndim)
    return pl.pallas_call(
        functools.partial(_glu_ln_kernel, alpha=alpha),
        grid=(m // tm,),
        in_specs=[rows, rows, whole(d_skip), whole(wv), whole(wg), whole(g), whole(b)],
        out_specs=rows,
        out_shape=jax.ShapeDtypeStruct((m, d), F32),
        compiler_params=_compiler_params(1),
        name="glu_ln",
    )(y, x, d_skip, wv, wg, g, b)


def _qkv_kernel(x_ref, w_ref, o_ref, xb_ref, *, n_q_tiles, scale):
    n = pl.program_id(1)

    @pl.when(n == 0)
    def _():
        xb_ref[...] = x_ref[...].astype(BF16)

    r = jnp.dot(xb_ref[...], w_ref[...], preferred_element_type=F32)
    r = r * jnp.where(n < n_q_tiles, scale, 1.0)
    o_ref[...] = r.astype(BF16)


def _qkv_proj(x, w, *, d_q, scale, tm, tn):
    m, d = x.shape
    n_out = w.shape[1]
    return pl.pallas_call(
        functools.partial(_qkv_kernel, n_q_tiles=d_q // tn, scale=scale),
        grid=(m // tm, n_out // tn),
        in_specs=[
            pl.BlockSpec((tm, d), lambda i, n: (i, 0)),
            pl.BlockSpec((d, tn), lambda i, n: (0, n)),
        ],
        out_specs=pl.BlockSpec((tm, tn), lambda i, n: (i, n)),
        out_shape=jax.ShapeDtypeStruct((m, n_out), BF16),
        scratch_shapes=[pltpu.VMEM((tm, d), BF16)],
        compiler_params=_compiler_params(2),
        name="qkv_proj",
    )(x, w)


NATTEN_ROWS_PER_ITER = 16


def _natten_kernel(q_ref, k_ref, v_ref, bias_ref, o_ref, *, grid_rows, kh, rows_per_iter):
    w = GRID_W

    def row_block(rb, carry):
        pending = []
        for u in range(rows_per_iter):
            r = rb * rows_per_iter + u
            rs = jnp.clip(r - kh // 2, 0, grid_rows - kh)
            q = q_ref[0, pl.ds(pl.multiple_of(r * w, w), w), :]
            k = k_ref[0, pl.ds(pl.multiple_of(rs * w, w), kh * w), :]
            s = lax.dot_general(q, k, NT_DIMS, preferred_element_type=F32)
            pending.append((r, rs, s + bias_ref[0, rs - r + (WIN_H - 1)]))
        for r, rs, s in pending:
            v = v_ref[0, pl.ds(pl.multiple_of(rs * w, w), kh * w), :]
            p = jnp.exp(s - jnp.max(s, axis=-1, keepdims=True))
            denom = jnp.sum(p, axis=-1, keepdims=True)
            o = jnp.dot(p.astype(BF16), v, preferred_element_type=F32) / denom
            o_ref[0, pl.ds(pl.multiple_of(r * w, w), w), :] = o.astype(BF16)
        return carry

    lax.fori_loop(0, grid_rows // rows_per_iter, row_block, 0)


def _natten(qkv, bias, *, kh):
    bsz, seq, three_d = qkv.shape
    n_heads = bias.shape[0]
    dh = three_d // (3 * n_heads)
    grid_rows = seq // GRID_W
    rows_per_iter = math.gcd(grid_rows, NATTEN_ROWS_PER_ITER)
    return pl.pallas_call(
        functools.partial(_natten_kernel, grid_rows=grid_rows, kh=kh, rows_per_iter=rows_per_iter),
        grid=(n_heads, bsz),
        in_specs=[
            pl.BlockSpec((1, seq, dh), lambda h, b: (b, 0, h)),
            pl.BlockSpec((1, seq, dh), lambda h, b: (b, 0, n_heads + h)),
            pl.BlockSpec((1, seq, dh), lambda h, b: (b, 0, 2 * n_heads + h)),
            pl.BlockSpec((1,) + bias.shape[1:], lambda h, b: (h, 0, 0, 0)),
        ],
        out_specs=pl.BlockSpec((1, seq, dh), lambda h, b: (b, 0, h)),
        out_shape=jax.ShapeDtypeStruct((bsz, seq, n_heads * dh), BF16),
        compiler_params=_compiler_params(2),
        name="natten",
    )(qkv, qkv, qkv, bias)


def _natten_bias(rpb, kh):
    cols = np.arange(GRID_W)
    col_start = np.clip(cols - WIN_W // 2, 0, GRID_W - WIN_W)
    kc = np.arange(GRID_W)
    in_win = (kc[None, :] >= col_start[:, None]) & (kc[None, :] < col_start[:, None] + WIN_W)
    col_off = kc[None, :] - cols[:, None] + (WIN_W - 1)
    onehot = ((col_off[None] == np.arange(2 * WIN_W - 1)[:, None, None]) & in_win[None]).astype(np.float32)
    n_off = 2 * WIN_H - kh
    r2 = jnp.stack([rpb.astype(F32)[:, o:o + kh] for o in range(n_off)], axis=1)
    b = jnp.einsum('hriw,wck->hrcik', r2, jnp.asarray(onehot), precision=HIGHEST)
    b = b + jnp.asarray(np.where(in_win, 0.0, MASK_VALUE), F32)[None, None, :, None, :]
    return b.reshape(b.shape[0], n_off, GRID_W, kh * GRID_W)


def _proj_ln_kernel(a_ref, x_ref, w_ref, g_ref, b_ref, o_ref, *, alpha):
    tm = o_ref.shape[0]
    sub = math.gcd(tm, EPILOGUE_SUB_ROWS)
    for r0 in range(0, tm, sub):
        z = alpha * x_ref[r0:r0 + sub, :] + jnp.dot(a_ref[r0:r0 + sub, :], w_ref[...], preferred_element_type=F32)
        o_ref[r0:r0 + sub, :] = _layer_norm_rows(z, g_ref[...], b_ref[...])


def _proj_ln(a, x, w, g, b, *, alpha, tm):
    m, d = x.shape
    whole = lambda v: pl.BlockSpec(v.shape, lambda i: (0,) * v.ndim)
    return pl.pallas_call(
        functools.partial(_proj_ln_kernel, alpha=alpha),
        grid=(m // tm,),
        in_specs=[pl.BlockSpec((tm, a.shape[1]), lambda i: (i, 0)), pl.BlockSpec((tm, d), lambda i: (i, 0)),
                  whole(w), whole(g), whole(b)],
        out_specs=pl.BlockSpec((tm, d), lambda i: (i, 0)),
        out_shape=jax.ShapeDtypeStruct((m, d), F32),
        compiler_params=_compiler_params(1),
        name="proj_ln",
    )(a, x, w, g, b)


def _tiles(m, d, bsz, seq):
    def rows(pref):
        return pref if m % pref == 0 else m
    s5_seqs = max(1, min(bsz, S5_TOKENS_PER_STEP // seq))
    while bsz % s5_seqs:
        s5_seqs -= 1
    return dict(tm=rows(512), ffn_tm=rows(1024), ffn_tf=512, qkv_tm=rows(1024), qkv_tn=d,
                s5_tokens=s5_seqs * seq, s5_sub=math.gcd(seq, S5_SUB_TOKENS))


def _cast_kernel(w_ref, o_ref, *, scale):
    w = w_ref[...]
    o_ref[...] = (w if scale is None else w * scale).astype(BF16)


def _cast_bf16(w, *, tr, scale=None):
    a, b, r, c = w.shape
    assert r % tr == 0
    spec = pl.BlockSpec((None, None, tr, c), lambda i, j, k: (i, j, k, 0))
    return pl.pallas_call(
        functools.partial(_cast_kernel, scale=scale),
        grid=(a, b, r // tr),
        in_specs=[spec],
        out_specs=spec,
        out_shape=jax.ShapeDtypeStruct(w.shape, BF16),
        compiler_params=_compiler_params(3),
        name="cast_bf16",
    )(w)


def _row_block(r, target):
    return max(t for t in range(16, target + 1, 16) if r % t == 0)


def kernel(x, ffn_w_gate, ffn_w_up, ffn_w_down, ln_g, ln_b, s5_lam_re, s5_lam_im, s5_log_dt, s5_b_re, s5_b_im, s5_c_re, s5_c_im, s5_d, s5_w_glu_val, s5_w_glu_gate, na_w_qkv, na_rpb, na_w_out):
    bsz, seq, d = x.shape
    depth = ffn_w_gate.shape[0]
    alpha = (2 * depth) ** 0.25
    m = bsz * seq
    n_chunks = seq // S5_CHUNK
    t = _tiles(m, d, bsz, seq)
    tm = t["tm"]
    grid_rows = seq // GRID_W
    kh = min(WIN_H, grid_rows)
    dh = d // N_HEADS

    wg_all = _cast_bf16(ffn_w_gate, tr=_row_block(d, 256))
    wu_all = _cast_bf16(ffn_w_up, tr=_row_block(d, 256))
    wd_all = _cast_bf16(ffn_w_down, tr=_row_block(ffn_w_down.shape[2], 768), scale=0.5)
    g_all = ln_g[:, :, None, :]
    b_all = ln_b[:, :, None, :]

    def ffn(h, i, k):
        return _ffn_ln(h, wg_all, wu_all, wd_all, g_all, b_all, i, k, alpha=alpha, tm=t["ffn_tm"], tf=t["ffn_tf"])

    h = x.reshape(m, d)
    for i in range(depth):
        h = ffn(h, i, 0)
        jm = i // N_MIXERS
        g1, b1 = ln_g[i, 1][None], ln_b[i, 1][None]
        if i % N_MIXERS == 0:
            tables = _s5_tables(s5_lam_re[jm], s5_lam_im[jm], s5_log_dt[jm], s5_b_re[jm], s5_b_im[jm],
                                        s5_c_re[jm], s5_c_im[jm], n_chunks)
            y = _s5_ssm(h, _s5_row_shuffle(t["s5_sub"]), _s5_lane_shuffle(), *tables, n_chunks=n_chunks,
                        tokens=t["s5_tokens"])
            h = _glu_ln(y, h, s5_d[jm][None], s5_w_glu_val[jm].astype(BF16), s5_w_glu_gate[jm].astype(BF16),
                        g1, b1, alpha=alpha, tm=tm)
        else:
            qkv = _qkv_proj(h, na_w_qkv[jm].astype(BF16), d_q=d, scale=dh ** -0.5, tm=t["qkv_tm"], tn=t["qkv_tn"])
            att = _natten(qkv.reshape(bsz, seq, 3 * d), _natten_bias(na_rpb[jm], kh), kh=kh)
            h = _proj_ln(att.reshape(m, d), h, na_w_out[jm].astype(BF16), g1, b1, alpha=alpha, tm=tm)
        h = ffn(h, i, 1)
    return h.reshape(bsz, seq, d)
```

```python
import functools
import math

import jax
import jax.numpy as jnp
import numpy as np
from jax import lax
from jax.experimental import pallas as pl
from jax.experimental.pallas import tpu as pltpu

GRID_W = 64
N_HEADS = 16
WIN_H = 8
WIN_W = 16
S5_GROUP = 16
N_MIXERS = 2
LN_EPS = 1e-5
MIN_NEG_RE = -1e-4

LANES = 128
VMEM_BYTES_V7X = 64 * 1024 * 1024
VMEM_LIMIT_BYTES = VMEM_BYTES_V7X - 2 * 1024 * 1024

FFN_SUB_ROWS = 512
EPILOGUE_SUB_ROWS = 256
S5_CHUNK = 16
S5_SUB_TOKENS = 512
S5_TOKENS_PER_STEP = 8192
MASK_VALUE = -1e30

F32 = jnp.float32
BF16 = jnp.bfloat16
HIGHEST = lax.Precision.HIGHEST
NT_DIMS = (((1,), (1,)), ((), ()))


def _compiler_params(n_axes):
    return pltpu.CompilerParams(dimension_semantics=("arbitrary",) * n_axes,
                                vmem_limit_bytes=VMEM_LIMIT_BYTES)


def _layer_norm_rows(z, g, b):
    mu = jnp.mean(z, axis=-1, keepdims=True)
    zc = z - mu
    var = jnp.mean(zc * zc, axis=-1, keepdims=True)
    return zc * lax.rsqrt(var + LN_EPS) * g + b


def _ffn_ln_kernel(x_ref, wg_ref, wu_ref, wd_ref, g_ref, b_ref, o_ref, xb_ref, *, nf, tail, alpha):
    f = pl.program_id(1)
    tm = o_ref.shape[0]
    sub = math.gcd(tm, FFN_SUB_ROWS)

    def chunk(xb, width):
        gate = jnp.dot(xb, wg_ref[:, :width], preferred_element_type=F32)
        up = jnp.dot(xb, wu_ref[:, :width], preferred_element_type=F32)
        h = (gate * jax.nn.sigmoid(gate) * up).astype(BF16)
        return jnp.dot(h, wd_ref[:width, :], preferred_element_type=F32)

    @pl.when(f == 0)
    def _():
        for r0 in range(0, tm, sub):
            xb = x_ref[r0:r0 + sub, :].astype(BF16)
            xb_ref[r0:r0 + sub, :] = xb
            o_ref[r0:r0 + sub, :] = chunk(xb, wg_ref.shape[1])

    @pl.when(jnp.logical_and(f > 0, f < nf - 1))
    def _():
        for r0 in range(0, tm, sub):
            o_ref[r0:r0 + sub, :] += chunk(xb_ref[r0:r0 + sub, :], wg_ref.shape[1])

    @pl.when(f == nf - 1)
    def _():
        for r0 in range(0, tm, sub):
            z = alpha * x_ref[r0:r0 + sub, :] + (o_ref[r0:r0 + sub, :] + chunk(xb_ref[r0:r0 + sub, :], tail))
            o_ref[r0:r0 + sub, :] = _layer_norm_rows(z, g_ref[...], b_ref[...])


def _ffn_ln(x, wg, wu, wd, g, b, layer, k, *, alpha, tm, tf):
    m, d = x.shape
    n_hidden = wg.shape[-1]
    nf = pl.cdiv(n_hidden, tf)
    tail = n_hidden - (nf - 1) * tf
    assert nf >= 2 and tail % LANES == 0
    return pl.pallas_call(
        functools.partial(_ffn_ln_kernel, nf=nf, tail=tail, alpha=alpha),
        grid=(m // tm, nf),
        in_specs=[
            pl.BlockSpec((tm, d), lambda i, f: (i, 0)),
            pl.BlockSpec((None, None, d, tf), lambda i, f: (layer, k, 0, f)),
            pl.BlockSpec((None, None, d, tf), lambda i, f: (layer, k, 0, f)),
            pl.BlockSpec((None, None, tf, d), lambda i, f: (layer, k, f, 0)),
            pl.BlockSpec((None, None, 1, d), lambda i, f: (layer, 2 * k, 0, 0)),
            pl.BlockSpec((None, None, 1, d), lambda i, f: (layer, 2 * k, 0, 0)),
        ],
        out_specs=pl.BlockSpec((tm, d), lambda i, f: (i, 0)),
        out_shape=jax.ShapeDtypeStruct((m, d), F32),
        scratch_shapes=[pltpu.VMEM((tm, d), BF16)],
        compiler_params=_compiler_params(2),
        name="ffn_ln",
    )(x, wg, wu, wd, g, b)


def _s5_kernel(x_ref, rp_ref, rpt_ref, p_ref, w1_ref, w2_ref, a1_ref, a2_ref, y_ref, *, n_chunks):
    sub = rp_ref.shape[0]
    cps = sub // S5_CHUNK
    subs_per_seq = n_chunks // cps
    n_seq = x_ref.shape[0] // (n_chunks * S5_CHUNK)
    groups = w1_ref.shape[0]
    width = w1_ref.shape[1]
    n_state = (w1_ref.shape[2] - width) // 2
    half = n_state // 2
    n_scan = n_chunks.bit_length() - 1
    j = lax.broadcasted_iota(jnp.int32, (n_chunks, n_state), 0)

    def cmul(a1, a2, v):
        return a1 * v + a2 * pltpu.roll(v, half, axis=1)

    def project(q):
        stacked = []
        for sb in range(q * subs_per_seq, (q + 1) * subs_per_seq):
            xs = x_ref[sb * sub:(sb + 1) * sub, :].astype(BF16)
            stacked.append(jnp.dot(rp_ref[...], xs, preferred_element_type=F32).astype(BF16))
        ucat = jnp.concatenate(
            [jnp.concatenate([st[s * cps:(s + 1) * cps] for st in stacked], axis=0) for s in range(S5_CHUNK)],
            axis=1)
        uperm = jnp.dot(ucat, p_ref[...], preferred_element_type=F32).astype(BF16)
        return [jnp.dot(uperm[:, gl * width:(gl + 1) * width], w1_ref[gl], preferred_element_type=F32)
                for gl in range(groups)]

    def scan(r1s):
        s_ins = []
        for gl, r1 in enumerate(r1s):
            s_fwd = r1[:, width:width + n_state]
            s_bwd = r1[:, width + n_state:]
            xf = jnp.where(j >= 1, pltpu.roll(s_fwd, 1, axis=0), 0.0)
            xb = jnp.where(j < n_chunks - 1, pltpu.roll(s_bwd, n_chunks - 1, axis=0), 0.0)
            for m in range(n_scan):
                sh = 1 << m
                sf = jnp.where(j >= sh, pltpu.roll(xf, sh, axis=0), 0.0)
                xf = xf + cmul(a1_ref[gl, 0, m:m + 1, :], a2_ref[gl, 0, m:m + 1, :], sf)
                sb = jnp.where(j < n_chunks - sh, pltpu.roll(xb, n_chunks - sh, axis=0), 0.0)
                xb = xb + cmul(a1_ref[gl, 1, m:m + 1, :], a2_ref[gl, 1, m:m + 1, :], sb)
            s_ins.append(jnp.concatenate([xf, xb], axis=1).astype(BF16))
        return s_ins

    def readout(q, r1s, s_ins):
        ys = [(r1[:, :width] + lax.dot_general(s_in, w2_ref[gl], NT_DIMS, preferred_element_type=F32)).astype(BF16)
              for gl, (r1, s_in) in enumerate(zip(r1s, s_ins))]
        yperm = lax.dot_general(jnp.concatenate(ys, axis=1), p_ref[...], NT_DIMS, preferred_element_type=F32)
        for k in range(subs_per_seq):
            ystack = jnp.concatenate([yperm[k * cps:(k + 1) * cps, s * LANES:(s + 1) * LANES].astype(BF16)
                                      for s in range(S5_CHUNK)], axis=0)
            sb = q * subs_per_seq + k
            y_ref[sb * sub:(sb + 1) * sub, :] = jnp.dot(rpt_ref[...], ystack, preferred_element_type=F32)

    r1s = [project(q) for q in range(n_seq)]
    for q in range(n_seq):
        readout(q, r1s[q], scan(r1s[q]))


def _s5_ssm(x, row_perm, lane_perm, w1, w2, a1, a2, *, n_chunks, tokens):
    m, d = x.shape
    gpt = LANES // S5_GROUP
    const2 = lambda t, r: (0, 0)
    return pl.pallas_call(
        functools.partial(_s5_kernel, n_chunks=n_chunks),
        grid=(d // LANES, m // tokens),
        in_specs=[
            pl.BlockSpec((tokens, LANES), lambda t, r: (r, t)),
            pl.BlockSpec(row_perm.shape, const2),
            pl.BlockSpec(row_perm.shape, const2),
            pl.BlockSpec(lane_perm.shape, const2),
            pl.BlockSpec((gpt,) + w1.shape[1:], lambda t, r: (t, 0, 0)),
            pl.BlockSpec((gpt,) + w2.shape[1:], lambda t, r: (t, 0, 0)),
            pl.BlockSpec((gpt,) + a1.shape[1:], lambda t, r: (t, 0, 0, 0)),
            pl.BlockSpec((gpt,) + a2.shape[1:], lambda t, r: (t, 0, 0, 0)),
        ],
        out_specs=pl.BlockSpec((tokens, LANES), lambda t, r: (r, t)),
        out_shape=jax.ShapeDtypeStruct((m, d), F32),
        compiler_params=_compiler_params(2),
        name="s5_ssm",
    )(x, row_perm, row_perm.T, lane_perm, w1, w2, a1, a2)


def _s5_row_shuffle(sub):
    cps = sub // S5_CHUNK
    dst = np.arange(sub)
    src = (dst % cps) * S5_CHUNK + dst // cps
    col = lax.broadcasted_iota(jnp.int32, (sub, sub), 1)
    return (col == jnp.asarray(src, jnp.int32)[:, None]).astype(BF16)


def _s5_lane_shuffle():
    gpt = LANES // S5_GROUP
    n = S5_CHUNK * LANES
    src = np.arange(n)
    step, grp, ch = src // LANES, (src % LANES) // S5_GROUP, src % S5_GROUP
    dst = grp * (S5_CHUNK * S5_GROUP) + step * S5_GROUP + ch
    assert gpt * S5_CHUNK * S5_GROUP == n
    col = lax.broadcasted_iota(jnp.int32, (n, n), 1)
    return (col == jnp.asarray(dst, jnp.int32)[:, None]).astype(BF16)


def _s5_tables(lam_re, lam_im, log_dt, b_re, b_im, c_re, c_im, n_chunks):
    t = S5_CHUNK
    n_groups, p = lam_re.shape[1:]
    gc = b_re.shape[-1]
    lr = jnp.minimum(lam_re.astype(F32), MIN_NEG_RE)
    li = lam_im.astype(F32)
    dt = jnp.exp(log_dt.astype(F32))[..., None]
    x, th = lr * dt, li * dt
    mag = jnp.exp(x)
    lb_re, lb_im = mag * jnp.cos(th), mag * jnp.sin(th)
    den = lr * lr + li * li
    nr, ni = lb_re - 1.0, lb_im
    f_re = (nr * lr + ni * li) / den
    f_im = (ni * lr - nr * li) / den
    brt = jnp.swapaxes(b_re.astype(F32), -1, -2)
    bit = jnp.swapaxes(b_im.astype(F32), -1, -2)
    bb_re = f_re[:, :, None] * brt - f_im[:, :, None] * bit
    bb_im = f_re[:, :, None] * bit + f_im[:, :, None] * brt
    cr, ci = c_re.astype(F32), c_im.astype(F32)

    kk = jnp.arange(t + 1, dtype=F32)[None, None, :, None, None]
    pmag = jnp.exp(kk * x[:, :, None, None, :])
    pw_re = pmag * jnp.cos(kk * th[:, :, None, None, :])
    pw_im = pmag * jnp.sin(kk * th[:, :, None, None, :])

    cp = jnp.concatenate([cr[:, :, None] * pw_re - ci[:, :, None] * pw_im,
                          -(cr[:, :, None] * pw_im + ci[:, :, None] * pw_re)], axis=-1)
    pb = jnp.concatenate([pw_re * bb_re[:, :, None] - pw_im * bb_im[:, :, None],
                          pw_re * bb_im[:, :, None] + pw_im * bb_re[:, :, None]], axis=-1)
    kern = jnp.einsum('dgkcp,dgep->dgkce', cp[:, :, :t], pb[:, :, 0], precision=HIGHEST)
    kern = jnp.concatenate([kern[0], kern[1]], axis=1)

    sig = np.arange(t)[:, None]
    tau = np.arange(t)[None, :]
    sel = np.zeros((2 * t, t, t), np.float32)
    for k in range(t):
        sel[k] = (tau - sig == k)
        sel[t + k] = (sig - tau == k)
    toep = jnp.einsum('kst,gkce->gsetc', jnp.asarray(sel), kern, precision=HIGHEST)
    toep = toep.reshape(n_groups, t * gc, t * gc)

    sin_f = pb[0][:, t - 1::-1][:, :t].reshape(n_groups, t * gc, 2 * p)
    sin_b = pb[1][:, :t].reshape(n_groups, t * gc, 2 * p)
    w1 = jnp.concatenate([toep, sin_f, sin_b], axis=2)
    out_f = cp[0][:, 1:t + 1].reshape(n_groups, t * gc, 2 * p)
    out_b = cp[1][:, t:0:-1].reshape(n_groups, t * gc, 2 * p)
    w2 = jnp.concatenate([out_f, out_b], axis=2)

    a_re, a_im = pw_re[:, :, t, 0], pw_im[:, :, t, 0]
    a1, a2 = [], []
    for _ in range(n_chunks.bit_length() - 1):
        a1.append(jnp.concatenate([a_re, a_re], axis=-1))
        a2.append(jnp.concatenate([-a_im, a_im], axis=-1))
        a_re, a_im = a_re * a_re - a_im * a_im, 2.0 * a_re * a_im
    a1 = jnp.transpose(jnp.stack(a1, axis=2), (1, 0, 2, 3))
    a2 = jnp.transpose(jnp.stack(a2, axis=2), (1, 0, 2, 3))
    return w1.astype(BF16), w2.astype(BF16), a1, a2


def _glu_ln_kernel(y_ref, x_ref, d_ref, wv_ref, wg_ref, g_ref, b_ref, o_ref, *, alpha):
    tm = o_ref.shape[0]
    sub = math.gcd(tm, EPILOGUE_SUB_ROWS)
    for r0 in range(0, tm, sub):
        x = x_ref[r0:r0 + sub, :]
        a = jax.nn.gelu(y_ref[r0:r0 + sub, :] + d_ref[...] * x).astype(BF16)
        val = jnp.dot(a, wv_ref[...], preferred_element_type=F32)
        gate = jnp.dot(a, wg_ref[...], preferred_element_type=F32)
        z = alpha * x + val * jax.nn.sigmoid(gate)
        o_ref[r0:r0 + sub, :] = _layer_norm_rows(z, g_ref[...], b_ref[...])


def _glu_ln(y, x, d_skip, wv, wg, g, b, *, alpha, tm):
    m, d = x.shape
    rows = pl.BlockSpec((tm, d), lambda i: (i, 0))
    whole = lambda a: pl.BlockSpec(a.shape, lambda i: (0,) * a.ndim)
    return pl.pallas_call(
        functools.partial(_glu_ln_kernel, alpha=alpha),
        grid=(m // tm,),
        in_specs=[rows, rows, whole(d_skip), whole(wv), whole(wg), whole(g), whole(b)],
        out_specs=rows,
        out_shape=jax.ShapeDtypeStruct((m, d), F32),
        compiler_params=_compiler_params(1),
        name="glu_ln",
    )(y, x, d_skip, wv, wg, g, b)


def _qkv_kernel(x_ref, w_ref, o_ref, xb_ref, *, n_q_tiles, scale):
    n = pl.program_id(1)

    @pl.when(n == 0)
    def _():
        xb_ref[...] = x_ref[...].astype(BF16)

    r = jnp.dot(xb_ref[...], w_ref[...], preferred_element_type=F32)
    r = r * jnp.where(n < n_q_tiles, scale, 1.0)
    o_ref[...] = r.astype(BF16)


def _qkv_proj(x, w, *, d_q, scale, tm, tn):
    m, d = x.shape
    n_out = w.shape[1]
    return pl.pallas_call(
        functools.partial(_qkv_kernel, n_q_tiles=d_q // tn, scale=scale),
        grid=(m // tm, n_out // tn),
        in_specs=[
            pl.BlockSpec((tm, d), lambda i, n: (i, 0)),
            pl.BlockSpec((d, tn), lambda i, n: (0, n)),
        ],
        out_specs=pl.BlockSpec((tm, tn), lambda i, n: (i, n)),
        out_shape=jax.ShapeDtypeStruct((m, n_out), BF16),
        scratch_shapes=[pltpu.VMEM((tm, d), BF16)],
        compiler_params=_compiler_params(2),
        name="qkv_proj",
    )(x, w)


NATTEN_ROWS_PER_ITER = 16


def _natten_kernel(q_ref, k_ref, v_ref, bias_ref, o_ref, *, grid_rows, kh, rows_per_iter):
    w = GRID_W

    def row_block(rb, carry):
        pending = []
        for u in range(rows_per_iter):
            r = rb * rows_per_iter + u
            rs = jnp.clip(r - kh // 2, 0, grid_rows - kh)
            q = q_ref[0, pl.ds(pl.multiple_of(r * w, w), w), :]
            k = k_ref[0, pl.ds(pl.multiple_of(rs * w, w), kh * w), :]
            s = lax.dot_general(q, k, NT_DIMS, preferred_element_type=F32)
            pending.append((r, rs, s + bias_ref[0, rs - r + (WIN_H - 1)]))
        for r, rs, s in pending:
            v = v_ref[0, pl.ds(pl.multiple_of(rs * w, w), kh * w), :]
            p = jnp.exp(s - jnp.max(s, axis=-1, keepdims=True))
            denom = jnp.sum(p, axis=-1, keepdims=True)
            o = jnp.dot(p.astype(BF16), v, preferred_element_type=F32) / denom
            o_ref[0, pl.ds(pl.multiple_of(r * w, w), w), :] = o.astype(BF16)
        return carry

    lax.fori_loop(0, grid_rows // rows_per_iter, row_block, 0)


def _natten(qkv, bias, *, kh):
    bsz, seq, three_d = qkv.shape
    n_heads = bias.shape[0]
    dh = three_d // (3 * n_heads)
    grid_rows = seq // GRID_W
    rows_per_iter = math.gcd(grid_rows, NATTEN_ROWS_PER_ITER)
    return pl.pallas_call(
        functools.partial(_natten_kernel, grid_rows=grid_rows, kh=kh, rows_per_iter=rows_per_iter),
        grid=(n_heads, bsz),
        in_specs=[
            pl.BlockSpec((1, seq, dh), lambda h, b: (b, 0, h)),
            pl.BlockSpec((1, seq, dh), lambda h, b: (b, 0, n_heads + h)),
            pl.BlockSpec((1, seq, dh), lambda h, b: (b, 0, 2 * n_heads + h)),
            pl.BlockSpec((1,) + bias.shape[1:], lambda h, b: (h, 0, 0, 0)),
        ],
        out_specs=pl.BlockSpec((1, seq, dh), lambda h, b: (b, 0, h)),
        out_shape=jax.ShapeDtypeStruct((bsz, seq, n_heads * dh), BF16),
        compiler_params=_compiler_params(2),
        name="natten",
    )(qkv, qkv, qkv, bias)


def _natten_bias(rpb, kh):
    cols = np.arange(GRID_W)
    col_start = np.clip(cols - WIN_W // 2, 0, GRID_W - WIN_W)
    kc = np.arange(GRID_W)
    in_win = (kc[None, :] >= col_start[:, None]) & (kc[None, :] < col_start[:, None] + WIN_W)
    col_off = kc[None, :] - cols[:, None] + (WIN_W - 1)
    onehot = ((col_off[None] == np.arange(2 * WIN_W - 1)[:, None, None]) & in_win[None]).astype(np.float32)
    n_off = 2 * WIN_H - kh
    r2 = jnp.stack([rpb.astype(F32)[:, o:o + kh] for o in range(n_off)], axis=1)
    b = jnp.einsum('hriw,wck->hrcik', r2, jnp.asarray(onehot), precision=HIGHEST)
    b = b + jnp.asarray(np.where(in_win, 0.0, MASK_VALUE), F32)[None, None, :, None, :]
    return b.reshape(b.shape[0], n_off, GRID_W, kh * GRID_W)


def _proj_ln_kernel(a_ref, x_ref, w_ref, g_ref, b_ref, o_ref, *, alpha):
    tm = o_ref.shape[0]
    sub = math.gcd(tm, EPILOGUE_SUB_ROWS)
    for r0 in range(0, tm, sub):
        z = alpha * x_ref[r0:r0 + sub, :] + jnp.dot(a_ref[r0:r0 + sub, :], w_ref[...], preferred_element_type=F32)
        o_ref[r0:r0 + sub, :] = _layer_norm_rows(z, g_ref[...], b_ref[...])


def _proj_ln(a, x, w, g, b, *, alpha, tm):
    m, d = x.shape
    whole = lambda v: pl.BlockSpec(v.shape, lambda i: (0,) * v.ndim)
    return pl.pallas_call(
        functools.partial(_proj_ln_kernel, alpha=alpha),
        grid=(m // tm,),
        in_specs=[pl.BlockSpec((tm, a.shape[1]), lambda i: (i, 0)), pl.BlockSpec((tm, d), lambda i: (i, 0)),
                  whole(w), whole(g), whole(b)],
        out_specs=pl.BlockSpec((tm, d), lambda i: (i, 0)),
        out_shape=jax.ShapeDtypeStruct((m, d), F32),
        compiler_params=_compiler_params(1),
        name="proj_ln",
    )(a, x, w, g, b)


def _tiles(m, d, bsz, seq):
    def rows(pref):
        return pref if m % pref == 0 else m
    s5_seqs = max(1, min(bsz, S5_TOKENS_PER_STEP // seq))
    while bsz % s5_seqs:
        s5_seqs -= 1
    return dict(tm=rows(512), ffn_tm=rows(1024), ffn_tf=512, qkv_tm=rows(1024), qkv_tn=d,
                s5_tokens=s5_seqs * seq, s5_sub=math.gcd(seq, S5_SUB_TOKENS))


def _cast_kernel(w_ref, o_ref, *, scale):
    w = w_ref[...]
    o_ref[...] = (w if scale is None else w * scale).astype(BF16)


def _cast_bf16(w, *, tr, scale=None):
    a, b, r, c = w.shape
    assert r % tr == 0
    spec = pl.BlockSpec((None, None, tr, c), lambda i, j, k: (i, j, k, 0))
    return pl.pallas_call(
        functools.partial(_cast_kernel, scale=scale),
        grid=(a, b, r // tr),
        in_specs=[spec],
        out_specs=spec,
        out_shape=jax.ShapeDtypeStruct(w.shape, BF16),
        compiler_params=_compiler_params(3),
        name="cast_bf16",
    )(w)


def _row_block(r, target):
    return max(t for t in range(16, target + 1, 16) if r % t == 0)


def kernel(x, ffn_w_gate, ffn_w_up, ffn_w_down, ln_g, ln_b, s5_lam_re, s5_lam_im, s5_log_dt, s5_b_re, s5_b_im, s5_c_re, s5_c_im, s5_d, s5_w_glu_val, s5_w_glu_gate, na_w_qkv, na_rpb, na_w_out):
    bsz, seq, d = x.shape
    depth = ffn_w_gate.shape[0]
    alpha = (2 * depth) ** 0.25
    m = bsz * seq
    n_chunks = seq // S5_CHUNK
    t = _tiles(m, d, bsz, seq)
    tm = t["tm"]
    grid_rows = seq // GRID_W
    kh = min(WIN_H, grid_rows)
    dh = d // N_HEADS

    wg_all = _cast_bf16(ffn_w_gate, tr=_row_block(d, 256))
    wu_all = _cast_bf16(ffn_w_up, tr=_row_block(d, 256))
    wd_all = _cast_bf16(ffn_w_down, tr=_row_block(ffn_w_down.shape[2], 768), scale=0.5)
    g_all = ln_g[:, :, None, :]
    b_all = ln_b[:, :, None, :]

    def ffn(h, i, k):
        return _ffn_ln(h, wg_all, wu_all, wd_all, g_all, b_all, i, k, alpha=alpha, tm=t["ffn_tm"], tf=t["ffn_tf"])

    h = x.reshape(m, d)
    for i in range(depth):
        h = ffn(h, i, 0)
        jm = i // N_MIXERS
        g1, b1 = ln_g[i, 1][None], ln_b[i, 1][None]
        if i % N_MIXERS == 0:
            w1, w2, a1, a2 = _s5_tables(s5_lam_re[jm], s5_lam_im[jm], s5_log_dt[jm], s5_b_re[jm], s5_b_im[jm],
                                        s5_c_re[jm], s5_c_im[jm], n_chunks)
            y = _s5_ssm(h, _s5_row_shuffle(t["s5_sub"]), _s5_lane_shuffle(), w1, w2, a1, a2, n_chunks=n_chunks,
                        tokens=t["s5_tokens"])
            h = _glu_ln(y, h, s5_d[jm][None], s5_w_glu_val[jm].astype(BF16), s5_w_glu_gate[jm].astype(BF16),
                        g1, b1, alpha=alpha, tm=tm)
        else:
            qkv = _qkv_proj(h, na_w_qkv[jm].astype(BF16), d_q=d, scale=dh ** -0.5, tm=t["qkv_tm"], tn=t["qkv_tn"])
            att = _natten(qkv.reshape(bsz, seq, 3 * d), _natten_bias(na_rpb[jm], kh), kh=kh)
            h = _proj_ln(att.reshape(m, d), h, na_w_out[jm].astype(BF16), g1, b1, alpha=alpha, tm=tm)
        h = ffn(h, i, 1)
    return h.reshape(bsz, seq, d)
```

```python
import functools
import math

import jax
import jax.numpy as jnp
import numpy as np
from jax import lax
from jax.experimental import pallas as pl
from jax.experimental.pallas import tpu as pltpu

GRID_W = 64
N_HEADS = 16
WIN_H = 8
WIN_W = 16
S5_GROUP = 16
N_MIXERS = 2
LN_EPS = 1e-5
MIN_NEG_RE = -1e-4

LANES = 128
VMEM_BYTES_V7X = 64 * 1024 * 1024
VMEM_LIMIT_BYTES = VMEM_BYTES_V7X - 2 * 1024 * 1024

FFN_SUB_ROWS = 512
EPILOGUE_SUB_ROWS = 256
S5_CHUNK = 16
S5_SUB_TOKENS = 512
S5_TOKENS_PER_STEP = 8192
MASK_VALUE = -1e30

F32 = jnp.float32
BF16 = jnp.bfloat16
HIGHEST = lax.Precision.HIGHEST
NT_DIMS = (((1,), (1,)), ((), ()))


def _compiler_params(n_axes):
    return pltpu.CompilerParams(dimension_semantics=("arbitrary",) * n_axes,
                                vmem_limit_bytes=VMEM_LIMIT_BYTES)


def _layer_norm_rows(z, g, b):
    mu = jnp.mean(z, axis=-1, keepdims=True)
    zc = z - mu
    var = jnp.mean(zc * zc, axis=-1, keepdims=True)
    return zc * lax.rsqrt(var + LN_EPS) * g + b


def _ffn_ln_kernel(x_ref, wg_ref, wu_ref, wd_ref, g_ref, b_ref, o_ref, xb_ref, *, nf, tail, alpha):
    f = pl.program_id(1)
    tm = o_ref.shape[0]
    sub = math.gcd(tm, FFN_SUB_ROWS)

    def chunk(xb, width):
        gate = jnp.dot(xb, wg_ref[:, :width], preferred_element_type=F32)
        up = jnp.dot(xb, wu_ref[:, :width], preferred_element_type=F32)
        h = (gate * jax.nn.sigmoid(gate) * up).astype(BF16)
        return jnp.dot(h, wd_ref[:width, :], preferred_element_type=F32)

    @pl.when(f == 0)
    def _():
        for r0 in range(0, tm, sub):
            xb = x_ref[r0:r0 + sub, :].astype(BF16)
            xb_ref[r0:r0 + sub, :] = xb
            o_ref[r0:r0 + sub, :] = chunk(xb, wg_ref.shape[1])

    @pl.when(jnp.logical_and(f > 0, f < nf - 1))
    def _():
        for r0 in range(0, tm, sub):
            o_ref[r0:r0 + sub, :] += chunk(xb_ref[r0:r0 + sub, :], wg_ref.shape[1])

    @pl.when(f == nf - 1)
    def _():
        for r0 in range(0, tm, sub):
            z = alpha * x_ref[r0:r0 + sub, :] + (o_ref[r0:r0 + sub, :] + chunk(xb_ref[r0:r0 + sub, :], tail))
            o_ref[r0:r0 + sub, :] = _layer_norm_rows(z, g_ref[...], b_ref[...])


def _ffn_ln(x, wg, wu, wd, g, b, layer, k, *, alpha, tm, tf):
    m, d = x.shape
    n_hidden = wg.shape[-1]
    nf = pl.cdiv(n_hidden, tf)
    tail = n_hidden - (nf - 1) * tf
    assert nf >= 2 and tail % LANES == 0
    return pl.pallas_call(
        functools.partial(_ffn_ln_kernel, nf=nf, tail=tail, alpha=alpha),
        grid=(m // tm, nf),
        in_specs=[
            pl.BlockSpec((tm, d), lambda i, f: (i, 0)),
            pl.BlockSpec((None, None, d, tf), lambda i, f: (layer, k, 0, f)),
            pl.BlockSpec((None, None, d, tf), lambda i, f: (layer, k, 0, f)),
            pl.BlockSpec((None, None, tf, d), lambda i, f: (layer, k, f, 0)),
            pl.BlockSpec((None, None, 1, d), lambda i, f: (layer, 2 * k, 0, 0)),
            pl.BlockSpec((None, None, 1, d), lambda i, f: (layer, 2 * k, 0, 0)),
        ],
        out_specs=pl.BlockSpec((tm, d), lambda i, f: (i, 0)),
        out_shape=jax.ShapeDtypeStruct((m, d), F32),
        scratch_shapes=[pltpu.VMEM((tm, d), BF16)],
        compiler_params=_compiler_params(2),
        name="ffn_ln",
    )(x, wg, wu, wd, g, b)


def _s5_kernel(x_ref, rp_ref, rpt_ref, p_ref, w1_ref, w2_ref, a1_ref, a2_ref, y_ref, *, n_chunks):
    sub = rp_ref.shape[0]
    cps = sub // S5_CHUNK
    subs_per_seq = n_chunks // cps
    n_seq = x_ref.shape[0] // (n_chunks * S5_CHUNK)
    groups = w1_ref.shape[0]
    width = w1_ref.shape[1]
    n_state = (w1_ref.shape[2] - width) // 2
    half = n_state // 2
    n_scan = n_chunks.bit_length() - 1
    j = lax.broadcasted_iota(jnp.int32, (n_chunks, n_state), 0)

    def cmul(a1, a2, v):
        return a1 * v + a2 * pltpu.roll(v, half, axis=1)

    def project(q):
        stacked = []
        for sb in range(q * subs_per_seq, (q + 1) * subs_per_seq):
            xs = x_ref[sb * sub:(sb + 1) * sub, :].astype(BF16)
            stacked.append(jnp.dot(rp_ref[...], xs, preferred_element_type=F32).astype(BF16))
        ucat = jnp.concatenate(
            [jnp.concatenate([st[s * cps:(s + 1) * cps] for st in stacked], axis=0) for s in range(S5_CHUNK)],
            axis=1)
        uperm = jnp.dot(ucat, p_ref[...], preferred_element_type=F32).astype(BF16)
        return [jnp.dot(uperm[:, gl * width:(gl + 1) * width], w1_ref[gl], preferred_element_type=F32)
                for gl in range(groups)]

    def scan(r1s):
        s_ins = []
        for gl, r1 in enumerate(r1s):
            s_fwd = r1[:, width:width + n_state]
            s_bwd = r1[:, width + n_state:]
            xf = jnp.where(j >= 1, pltpu.roll(s_fwd, 1, axis=0), 0.0)
            xb = jnp.where(j < n_chunks - 1, pltpu.roll(s_bwd, n_chunks - 1, axis=0), 0.0)
            for m in range(n_scan):
                sh = 1 << m
                sf = jnp.where(j >= sh, pltpu.roll(xf, sh, axis=0), 0.0)
                xf = xf + cmul(a1_ref[gl, 0, m:m + 1, :], a2_ref[gl, 0, m:m + 1, :], sf)
                sb = jnp.where(j < n_chunks - sh, pltpu.roll(xb, n_chunks - sh, axis=0), 0.0)
                xb = xb + cmul(a1_ref[gl, 1, m:m + 1, :], a2_ref[gl, 1, m:m + 1, :], sb)
            s_ins.append(jnp.concatenate([xf, xb], axis=1).astype(BF16))
        return s_ins

    def readout(q, r1s, s_ins):
        ys = [(r1[:, :width] + lax.dot_general(s_in, w2_ref[gl], NT_DIMS, preferred_element_type=F32)).astype(BF16)
              for gl, (r1, s_in) in enumerate(zip(r1s, s_ins))]
        yperm = lax.dot_general(jnp.concatenate(ys, axis=1), p_ref[...], NT_DIMS, preferred_element_type=F32)
        for k in range(subs_per_seq):
            ystack = jnp.concatenate([yperm[k * cps:(k + 1) * cps, s * LANES:(s + 1) * LANES].astype(BF16)
                                      for s in range(S5_CHUNK)], axis=0)
            sb = q * subs_per_seq + k
            y_ref[sb * sub:(sb + 1) * sub, :] = jnp.dot(rpt_ref[...], ystack, preferred_element_type=F32)

    r1s = [project(q) for q in range(n_seq)]
    for q in range(n_seq):
        readout(q, r1s[q], scan(r1s[q]))


def _s5_ssm(x, row_perm, lane_perm, w1, w2, a1, a2, *, n_chunks, tokens):
    m, d = x.shape
    gpt = LANES // S5_GROUP
    const2 = lambda t, r: (0, 0)
    return pl.pallas_call(
        functools.partial(_s5_kernel, n_chunks=n_chunks),
        grid=(d // LANES, m // tokens),
        in_specs=[
            pl.BlockSpec((tokens, LANES), lambda t, r: (r, t)),
            pl.BlockSpec(row_perm.shape, const2),
            pl.BlockSpec(row_perm.shape, const2),
            pl.BlockSpec(lane_perm.shape, const2),
            pl.BlockSpec((gpt,) + w1.shape[1:], lambda t, r: (t, 0, 0)),
            pl.BlockSpec((gpt,) + w2.shape[1:], lambda t, r: (t, 0, 0)),
            pl.BlockSpec((gpt,) + a1.shape[1:], lambda t, r: (t, 0, 0, 0)),
            pl.BlockSpec((gpt,) + a2.shape[1:], lambda t, r: (t, 0, 0, 0)),
        ],
        out_specs=pl.BlockSpec((tokens, LANES), lambda t, r: (r, t)),
        out_shape=jax.ShapeDtypeStruct((m, d), F32),
        compiler_params=_compiler_params(2),
        name="s5_ssm",
    )(x, row_perm, row_perm.T, lane_perm, w1, w2, a1, a2)


def _s5_row_shuffle(sub):
    cps = sub // S5_CHUNK
    dst = np.arange(sub)
    src = (dst % cps) * S5_CHUNK + dst // cps
    col = lax.broadcasted_iota(jnp.int32, (sub, sub), 1)
    return (col == jnp.asarray(src, jnp.int32)[:, None]).astype(BF16)


def _s5_lane_shuffle():
    gpt = LANES // S5_GROUP
    n = S5_CHUNK * LANES
    src = np.arange(n)
    step, grp, ch = src // LANES, (src % LANES) // S5_GROUP, src % S5_GROUP
    dst = grp * (S5_CHUNK * S5_GROUP) + step * S5_GROUP + ch
    assert gpt * S5_CHUNK * S5_GROUP == n
    col = lax.broadcasted_iota(jnp.int32, (n, n), 1)
    return (col == jnp.asarray(dst, jnp.int32)[:, None]).astype(BF16)


def _s5_tables(lam_re, lam_im, log_dt, b_re, b_im, c_re, c_im, n_chunks):
    t = S5_CHUNK
    n_groups, p = lam_re.shape[1:]
    gc = b_re.shape[-1]
    lr = jnp.minimum(lam_re.astype(F32), MIN_NEG_RE)
    li = lam_im.astype(F32)
    dt = jnp.exp(log_dt.astype(F32))[..., None]
    x, th = lr * dt, li * dt
    mag = jnp.exp(x)
    lb_re, lb_im = mag * jnp.cos(th), mag * jnp.sin(th)
    den = lr * lr + li * li
    nr, ni = lb_re - 1.0, lb_im
    f_re = (nr * lr + ni * li) / den
    f_im = (ni * lr - nr * li) / den
    brt = jnp.swapaxes(b_re.astype(F32), -1, -2)
    bit = jnp.swapaxes(b_im.astype(F32), -1, -2)
    bb_re = f_re[:, :, None] * brt - f_im[:, :, None] * bit
    bb_im = f_re[:, :, None] * bit + f_im[:, :, None] * brt
    cr, ci = c_re.astype(F32), c_im.astype(F32)

    kk = jnp.arange(t + 1, dtype=F32)[None, None, :, None, None]
    pmag = jnp.exp(kk * x[:, :, None, None, :])
    pw_re = pmag * jnp.cos(kk * th[:, :, None, None, :])
    pw_im = pmag * jnp.sin(kk * th[:, :, None, None, :])

    cp = jnp.concatenate([cr[:, :, None] * pw_re - ci[:, :, None] * pw_im,
                          -(cr[:, :, None] * pw_im + ci[:, :, None] * pw_re)], axis=-1)
    pb = jnp.concatenate([pw_re * bb_re[:, :, None] - pw_im * bb_im[:, :, None],
                          pw_re * bb_im[:, :, None] + pw_im * bb_re[:, :, None]], axis=-1)
    kern = jnp.einsum('dgkcp,dgep->dgkce', cp[:, :, :t], pb[:, :, 0], precision=HIGHEST)
    kern = jnp.concatenate([kern[0], kern[1]], axis=1)

    sig = np.arange(t)[:, None]
    tau = np.arange(t)[None, :]
    sel = np.zeros((2 * t, t, t), np.float32)
    for k in range(t):
        sel[k] = (tau - sig == k)
        sel[t + k] = (sig - tau == k)
    toep = jnp.einsum('kst,gkce->gsetc', jnp.asarray(sel), kern, precision=HIGHEST)
    toep = toep.reshape(n_groups, t * gc, t * gc)

    sin_f = pb[0][:, t - 1::-1][:, :t].reshape(n_groups, t * gc, 2 * p)
    sin_b = pb[1][:, :t].reshape(n_groups, t * gc, 2 * p)
    w1 = jnp.concatenate([toep, sin_f, sin_b], axis=2)
    out_f = cp[0][:, 1:t + 1].reshape(n_groups, t * gc, 2 * p)
    out_b = cp[1][:, t:0:-1].reshape(n_groups, t * gc, 2 * p)
    w2 = jnp.concatenate([out_f, out_b], axis=2)

    a_re, a_im = pw_re[:, :, t, 0], pw_im[:, :, t, 0]
    a1, a2 = [], []
    for _ in range(n_chunks.bit_length() - 1):
        a1.append(jnp.concatenate([a_re, a_re], axis=-1))
        a2.append(jnp.concatenate([-a_im, a_im], axis=-1))
        a_re, a_im = a_re * a_re - a_im * a_im, 2.0 * a_re * a_im
    a1 = jnp.transpose(jnp.stack(a1, axis=2), (1, 0, 2, 3))
    a2 = jnp.transpose(jnp.stack(a2, axis=2), (1, 0, 2, 3))
    return w1.astype(BF16), w2.astype(BF16), a1, a2


def _glu_ln_kernel(y_ref, x_ref, d_ref, wv_ref, wg_ref, g_ref, b_ref, o_ref, *, alpha):
    tm = o_ref.shape[0]
    sub = math.gcd(tm, EPILOGUE_SUB_ROWS)
    for r0 in range(0, tm, sub):
        x = x_ref[r0:r0 + sub, :]
        a = jax.nn.gelu(y_ref[r0:r0 + sub, :] + d_ref[...] * x).astype(BF16)
        val = jnp.dot(a, wv_ref[...], preferred_element_type=F32)
        gate = jnp.dot(a, wg_ref[...], preferred_element_type=F32)
        z = alpha * x + val * jax.nn.sigmoid(gate)
        o_ref[r0:r0 + sub, :] = _layer_norm_rows(z, g_ref[...], b_ref[...])


def _glu_ln(y, x, d_skip, wv, wg, g, b, *, alpha, tm):
    m, d = x.shape
    rows = pl.BlockSpec((tm, d), lambda i: (i, 0))
    whole = lambda a: pl.BlockSpec(a.shape, lambda i: (0,) * a.ndim)
    return pl.pallas_call(
        functools.partial(_glu_ln_kernel, alpha=alpha),
        grid=(m // tm,),
        in_specs=[rows, rows, whole(d_skip), whole(wv), whole(wg), whole(g), whole(b)],
        out_specs=rows,
        out_shape=jax.ShapeDtypeStruct((m, d), F32),
        compiler_params=_compiler_params(1),
        name="glu_ln",
    )(y, x, d_skip, wv, wg, g, b)


def _qkv_kernel(x_ref, w_ref, o_ref, *, d_q, tn, scale):
    xb = x_ref[...].astype(BF16)
    for c0 in range(0, w_ref.shape[1], tn):
        r = jnp.dot(xb, w_ref[:, c0:c0 + tn], preferred_element_type=F32)
        if c0 < d_q:
            r = r * scale
        o_ref[:, c0:c0 + tn] = r.astype(BF16)


def _qkv_proj(x, w, *, d_q, scale, tm, tn):
    m, d = x.shape
    n_out = w.shape[1]
    assert d_q % tn == 0 and n_out % tn == 0
    return pl.pallas_call(
        functools.partial(_qkv_kernel, d_q=d_q, tn=tn, scale=scale),
        grid=(m // tm,),
        in_specs=[
            pl.BlockSpec((tm, d), lambda i: (i, 0)),
            pl.BlockSpec(w.shape, lambda i: (0, 0)),
        ],
        out_specs=pl.BlockSpec((tm, n_out), lambda i: (i, 0)),
        out_shape=jax.ShapeDtypeStruct((m, n_out), BF16),
        compiler_params=_compiler_params(1),
        name="qkv_proj",
    )(x, w)


NATTEN_ROWS_PER_ITER = 32


def _natten_kernel(q_ref, k_ref, v_ref, bias_ref, o_ref, *, grid_rows, kh, rows_per_iter):
    w = GRID_W

    def row_block(rb, carry):
        pending = []
        for u in range(rows_per_iter):
            r = rb * rows_per_iter + u
            rs = jnp.clip(r - kh // 2, 0, grid_rows - kh)
            q = q_ref[0, pl.ds(pl.multiple_of(r * w, w), w), :]
            k = k_ref[0, pl.ds(pl.multiple_of(rs * w, w), kh * w), :]
            s = lax.dot_general(q, k, NT_DIMS, preferred_element_type=F32)
            pending.append((r, rs, s + bias_ref[0, rs - r + (WIN_H - 1)]))
        for r, rs, s in pending:
            v = v_ref[0, pl.ds(pl.multiple_of(rs * w, w), kh * w), :]
            p = jnp.exp(s - jnp.max(s, axis=-1, keepdims=True))
            denom = jnp.sum(p, axis=-1, keepdims=True)
            o = jnp.dot(p.astype(BF16), v, preferred_element_type=F32) / denom
            o_ref[0, pl.ds(pl.multiple_of(r * w, w), w), :] = o.astype(BF16)
        return carry

    lax.fori_loop(0, grid_rows // rows_per_iter, row_block, 0)


def _natten(qkv, bias, *, kh):
    bsz, seq, three_d = qkv.shape
    n_heads = bias.shape[0]
    dh = three_d // (3 * n_heads)
    grid_rows = seq // GRID_W
    rows_per_iter = math.gcd(grid_rows, NATTEN_ROWS_PER_ITER)
    return pl.pallas_call(
        functools.partial(_natten_kernel, grid_rows=grid_rows, kh=kh, rows_per_iter=rows_per_iter),
        grid=(n_heads, bsz),
        in_specs=[
            pl.BlockSpec((1, seq, dh), lambda h, b: (b, 0, h)),
            pl.BlockSpec((1, seq, dh), lambda h, b: (b, 0, n_heads + h)),
            pl.BlockSpec((1, seq, dh), lambda h, b: (b, 0, 2 * n_heads + h)),
            pl.BlockSpec((1,) + bias.shape[1:], lambda h, b: (h, 0, 0, 0)),
        ],
        out_specs=pl.BlockSpec((1, seq, dh), lambda h, b: (b, 0, h)),
        out_shape=jax.ShapeDtypeStruct((bsz, seq, n_heads * dh), BF16),
        compiler_params=_compiler_params(2),
        name="natten",
    )(qkv, qkv, qkv, bias)


def _natten_bias(rpb, kh):
    cols = np.arange(GRID_W)
    col_start = np.clip(cols - WIN_W // 2, 0, GRID_W - WIN_W)
    kc = np.arange(GRID_W)
    in_win = (kc[None, :] >= col_start[:, None]) & (kc[None, :] < col_start[:, None] + WIN_W)
    col_off = kc[None, :] - cols[:, None] + (WIN_W - 1)
    onehot = ((col_off[None] == np.arange(2 * WIN_W - 1)[:, None, None]) & in_win[None]).astype(np.float32)
    n_off = 2 * WIN_H - kh
    r2 = jnp.stack([rpb.astype(F32)[:, o:o + kh] for o in range(n_off)], axis=1)
    b = jnp.einsum('hriw,wck->hrcik', r2, jnp.asarray(onehot), precision=HIGHEST)
    b = b + jnp.asarray(np.where(in_win, 0.0, MASK_VALUE), F32)[None, None, :, None, :]
    return b.reshape(b.shape[0], n_off, GRID_W, kh * GRID_W)


def _proj_ln_kernel(a_ref, x_ref, w_ref, g_ref, b_ref, o_ref, *, alpha):
    tm = o_ref.shape[0]
    sub = math.gcd(tm, EPILOGUE_SUB_ROWS)
    for r0 in range(0, tm, sub):
        z = alpha * x_ref[r0:r0 + sub, :] + jnp.dot(a_ref[r0:r0 + sub, :], w_ref[...], preferred_element_type=F32)
        o_ref[r0:r0 + sub, :] = _layer_norm_rows(z, g_ref[...], b_ref[...])


def _proj_ln(a, x, w, g, b, *, alpha, tm):
    m, d = x.shape
    whole = lambda v: pl.BlockSpec(v.shape, lambda i: (0,) * v.ndim)
    return pl.pallas_call(
        functools.partial(_proj_ln_kernel, alpha=alpha),
        grid=(m // tm,),
        in_specs=[pl.BlockSpec((tm, a.shape[1]), lambda i: (i, 0)), pl.BlockSpec((tm, d), lambda i: (i, 0)),
                  whole(w), whole(g), whole(b)],
        out_specs=pl.BlockSpec((tm, d), lambda i: (i, 0)),
        out_shape=jax.ShapeDtypeStruct((m, d), F32),
        compiler_params=_compiler_params(1),
        name="proj_ln",
    )(a, x, w, g, b)


def _tiles(m, d, bsz, seq):
    def rows(pref):
        return pref if m % pref == 0 else m
    s5_seqs = max(1, min(bsz, S5_TOKENS_PER_STEP // seq))
    while bsz % s5_seqs:
        s5_seqs -= 1
    return dict(tm=rows(512), ffn_tm=rows(1024), ffn_tf=512, qkv_tm=rows(512), qkv_tn=d,
                s5_tokens=s5_seqs * seq, s5_sub=math.gcd(seq, S5_SUB_TOKENS))


def _cast_kernel(w_ref, o_ref, *, scale):
    w = w_ref[...]
    o_ref[...] = (w if scale is None else w * scale).astype(BF16)


def _cast_bf16(w, *, tr, scale=None):
    a, b, r, c = w.shape
    assert r % tr == 0
    spec = pl.BlockSpec((None, None, tr, c), lambda i, j, k: (i, j, k, 0))
    return pl.pallas_call(
        functools.partial(_cast_kernel, scale=scale),
        grid=(a, b, r // tr),
        in_specs=[spec],
        out_specs=spec,
        out_shape=jax.ShapeDtypeStruct(w.shape, BF16),
        compiler_params=_compiler_params(3),
        name="cast_bf16",
    )(w)


def _row_block(r, target):
    return max(t for t in range(16, target + 1, 16) if r % t == 0)


def kernel(x, ffn_w_gate, ffn_w_up, ffn_w_down, ln_g, ln_b, s5_lam_re, s5_lam_im, s5_log_dt, s5_b_re, s5_b_im, s5_c_re, s5_c_im, s5_d, s5_w_glu_val, s5_w_glu_gate, na_w_qkv, na_rpb, na_w_out):
    bsz, seq, d = x.shape
    depth = ffn_w_gate.shape[0]
    alpha = (2 * depth) ** 0.25
    m = bsz * seq
    n_chunks = seq // S5_CHUNK
    t = _tiles(m, d, bsz, seq)
    tm = t["tm"]
    grid_rows = seq // GRID_W
    kh = min(WIN_H, grid_rows)
    dh = d // N_HEADS

    wg_all = _cast_bf16(ffn_w_gate, tr=_row_block(d, 256))
    wu_all = _cast_bf16(ffn_w_up, tr=_row_block(d, 256))
    wd_all = _cast_bf16(ffn_w_down, tr=_row_block(ffn_w_down.shape[2], 768), scale=0.5)
    g_all = ln_g[:, :, None, :]
    b_all = ln_b[:, :, None, :]

    def ffn(h, i, k):
        return _ffn_ln(h, wg_all, wu_all, wd_all, g_all, b_all, i, k, alpha=alpha, tm=t["ffn_tm"], tf=t["ffn_tf"])

    h = x.reshape(m, d)
    for i in range(depth):
        h = ffn(h, i, 0)
        jm = i // N_MIXERS
        g1, b1 = ln_g[i, 1][None], ln_b[i, 1][None]
        if i % N_MIXERS == 0:
            w1, w2, a1, a2 = _s5_tables(s5_lam_re[jm], s5_lam_im[jm], s5_log_dt[jm], s5_b_re[jm], s5_b_im[jm],
                                        s5_c_re[jm], s5_c_im[jm], n_chunks)
            y = _s5_ssm(h, _s5_row_shuffle(t["s5_sub"]), _s5_lane_shuffle(), w1, w2, a1, a2, n_chunks=n_chunks,
                        tokens=t["s5_tokens"])
            h = _glu_ln(y, h, s5_d[jm][None], s5_w_glu_val[jm].astype(BF16), s5_w_glu_gate[jm].astype(BF16),
                        g1, b1, alpha=alpha, tm=tm)
        else:
            qkv = _qkv_proj(h, na_w_qkv[jm].astype(BF16), d_q=d, scale=dh ** -0.5, tm=t["qkv_tm"], tn=t["qkv_tn"])
            att = _natten(qkv.reshape(bsz, seq, 3 * d), _natten_bias(na_rpb[jm], kh), kh=kh)
            h = _proj_ln(att.reshape(m, d), h, na_w_out[jm].astype(BF16), g1, b1, alpha=alpha, tm=tm)
        h = ffn(h, i, 1)
    return h.reshape(bsz, seq, d)
```

```python
import functools
import math

import jax
import jax.numpy as jnp
import numpy as np
from jax import lax
from jax.experimental import pallas as pl
from jax.experimental.pallas import tpu as pltpu

GRID_W = 64
N_HEADS = 16
WIN_H = 8
WIN_W = 16
S5_GROUP = 16
N_MIXERS = 2
LN_EPS = 1e-5
MIN_NEG_RE = -1e-4

LANES = 128
VMEM_BYTES_V7X = 64 * 1024 * 1024
VMEM_LIMIT_BYTES = VMEM_BYTES_V7X - 2 * 1024 * 1024

FFN_SUB_ROWS = 512
EPILOGUE_SUB_ROWS = 256
S5_CHUNK = 16
S5_SUB_TOKENS = 512
S5_TOKENS_PER_STEP = 8192
MASK_VALUE = -1e30

F32 = jnp.float32
BF16 = jnp.bfloat16
HIGHEST = lax.Precision.HIGHEST
NT_DIMS = (((1,), (1,)), ((), ()))


def _compiler_params(n_axes):
    return pltpu.CompilerParams(dimension_semantics=("arbitrary",) * n_axes,
                                vmem_limit_bytes=VMEM_LIMIT_BYTES)


def _layer_norm_rows(z, g, b):
    mu = jnp.mean(z, axis=-1, keepdims=True)
    zc = z - mu
    var = jnp.mean(zc * zc, axis=-1, keepdims=True)
    return zc * lax.rsqrt(var + LN_EPS) * g + b


def _ffn_ln_kernel(x_ref, wg_ref, wu_ref, wd_ref, g_ref, b_ref, o_ref, xb_ref, *, nf, tail, alpha):
    f = pl.program_id(1)
    tm = o_ref.shape[0]
    sub = math.gcd(tm, FFN_SUB_ROWS)

    def chunk(xb, width):
        gate = jnp.dot(xb, wg_ref[:, :width], preferred_element_type=F32)
        up = jnp.dot(xb, wu_ref[:, :width], preferred_element_type=F32)
        h = (gate * jax.nn.sigmoid(gate) * up).astype(BF16)
        return jnp.dot(h, wd_ref[:width, :], preferred_element_type=F32)

    @pl.when(f == 0)
    def _():
        for r0 in range(0, tm, sub):
            xb = x_ref[r0:r0 + sub, :].astype(BF16)
            xb_ref[r0:r0 + sub, :] = xb
            o_ref[r0:r0 + sub, :] = chunk(xb, wg_ref.shape[1])

    @pl.when(jnp.logical_and(f > 0, f < nf - 1))
    def _():
        for r0 in range(0, tm, sub):
            o_ref[r0:r0 + sub, :] += chunk(xb_ref[r0:r0 + sub, :], wg_ref.shape[1])

    @pl.when(f == nf - 1)
    def _():
        for r0 in range(0, tm, sub):
            z = alpha * x_ref[r0:r0 + sub, :] + (o_ref[r0:r0 + sub, :] + chunk(xb_ref[r0:r0 + sub, :], tail))
            o_ref[r0:r0 + sub, :] = _layer_norm_rows(z, g_ref[...], b_ref[...])


def _ffn_ln(x, wg, wu, wd, g, b, layer, k, *, alpha, tm, tf):
    m, d = x.shape
    n_hidden = wg.shape[-1]
    nf = pl.cdiv(n_hidden, tf)
    tail = n_hidden - (nf - 1) * tf
    assert nf >= 2 and tail % LANES == 0
    return pl.pallas_call(
        functools.partial(_ffn_ln_kernel, nf=nf, tail=tail, alpha=alpha),
        grid=(m // tm, nf),
        in_specs=[
            pl.BlockSpec((tm, d), lambda i, f: (i, 0)),
            pl.BlockSpec((None, None, d, tf), lambda i, f: (layer, k, 0, f)),
            pl.BlockSpec((None, None, d, tf), lambda i, f: (layer, k, 0, f)),
            pl.BlockSpec((None, None, tf, d), lambda i, f: (layer, k, f, 0)),
            pl.BlockSpec((None, None, 1, d), lambda i, f: (layer, 2 * k, 0, 0)),
            pl.BlockSpec((None, None, 1, d), lambda i, f: (layer, 2 * k, 0, 0)),
        ],
        out_specs=pl.BlockSpec((tm, d), lambda i, f: (i, 0)),
        out_shape=jax.ShapeDtypeStruct((m, d), F32),
        scratch_shapes=[pltpu.VMEM((tm, d), BF16)],
        compiler_params=_compiler_params(2),
        name="ffn_ln",
    )(x, wg, wu, wd, g, b)


def _s5_kernel(x_ref, rp_ref, rpt_ref, p_ref, w1_ref, w2_ref, a1_ref, a2_ref, y_ref, *, n_chunks):
    sub = rp_ref.shape[0]
    cps = sub // S5_CHUNK
    subs_per_seq = n_chunks // cps
    n_seq = x_ref.shape[0] // (n_chunks * S5_CHUNK)
    groups = w1_ref.shape[0]
    width = w1_ref.shape[1]
    n_state = (w1_ref.shape[2] - width) // 2
    half = n_state // 2
    n_scan = n_chunks.bit_length() - 1
    j = lax.broadcasted_iota(jnp.int32, (n_chunks, n_state), 0)

    def cmul(a1, a2, v):
        return a1 * v + a2 * pltpu.roll(v, half, axis=1)

    def project(q):
        base = q * n_chunks * S5_CHUNK
        ucat = jnp.concatenate([x_ref[pl.ds(base + s, n_chunks, stride=S5_CHUNK), :].astype(BF16)
                                for s in range(S5_CHUNK)], axis=1)
        uperm = jnp.dot(ucat, p_ref[...], preferred_element_type=F32).astype(BF16)
        return [jnp.dot(uperm[:, gl * width:(gl + 1) * width], w1_ref[gl], preferred_element_type=F32)
                for gl in range(groups)]

    def scan(r1s):
        s_ins = []
        for gl, r1 in enumerate(r1s):
            s_fwd = r1[:, width:width + n_state]
            s_bwd = r1[:, width + n_state:]
            xf = jnp.where(j >= 1, pltpu.roll(s_fwd, 1, axis=0), 0.0)
            xb = jnp.where(j < n_chunks - 1, pltpu.roll(s_bwd, n_chunks - 1, axis=0), 0.0)
            for m in range(n_scan):
                sh = 1 << m
                sf = jnp.where(j >= sh, pltpu.roll(xf, sh, axis=0), 0.0)
                xf = xf + cmul(a1_ref[gl, 0, m:m + 1, :], a2_ref[gl, 0, m:m + 1, :], sf)
                sb = jnp.where(j < n_chunks - sh, pltpu.roll(xb, n_chunks - sh, axis=0), 0.0)
                xb = xb + cmul(a1_ref[gl, 1, m:m + 1, :], a2_ref[gl, 1, m:m + 1, :], sb)
            s_ins.append(jnp.concatenate([xf, xb], axis=1).astype(BF16))
        return s_ins

    def readout(q, r1s, s_ins):
        ys = [(r1[:, :width] + lax.dot_general(s_in, w2_ref[gl], NT_DIMS, preferred_element_type=F32)).astype(BF16)
              for gl, (r1, s_in) in enumerate(zip(r1s, s_ins))]
        yperm = lax.dot_general(jnp.concatenate(ys, axis=1), p_ref[...], NT_DIMS, preferred_element_type=F32)
        base = q * n_chunks * S5_CHUNK
        for s in range(S5_CHUNK):
            y_ref[pl.ds(base + s, n_chunks, stride=S5_CHUNK), :] = yperm[:, s * LANES:(s + 1) * LANES]

    r1s = [project(q) for q in range(n_seq)]
    for q in range(n_seq):
        readout(q, r1s[q], scan(r1s[q]))


def _s5_ssm(x, row_perm, lane_perm, w1, w2, a1, a2, *, n_chunks, tokens):
    m, d = x.shape
    gpt = LANES // S5_GROUP
    const2 = lambda t, r: (0, 0)
    return pl.pallas_call(
        functools.partial(_s5_kernel, n_chunks=n_chunks),
        grid=(d // LANES, m // tokens),
        in_specs=[
            pl.BlockSpec((tokens, LANES), lambda t, r: (r, t)),
            pl.BlockSpec(row_perm.shape, const2),
            pl.BlockSpec(row_perm.shape, const2),
            pl.BlockSpec(lane_perm.shape, const2),
            pl.BlockSpec((gpt,) + w1.shape[1:], lambda t, r: (t, 0, 0)),
            pl.BlockSpec((gpt,) + w2.shape[1:], lambda t, r: (t, 0, 0)),
            pl.BlockSpec((gpt,) + a1.shape[1:], lambda t, r: (t, 0, 0, 0)),
            pl.BlockSpec((gpt,) + a2.shape[1:], lambda t, r: (t, 0, 0, 0)),
        ],
        out_specs=pl.BlockSpec((tokens, LANES), lambda t, r: (r, t)),
        out_shape=jax.ShapeDtypeStruct((m, d), F32),
        compiler_params=_compiler_params(2),
        name="s5_ssm",
    )(x, row_perm, row_perm.T, lane_perm, w1, w2, a1, a2)


def _s5_row_shuffle(sub):
    cps = sub // S5_CHUNK
    dst = np.arange(sub)
    src = (dst % cps) * S5_CHUNK + dst // cps
    col = lax.broadcasted_iota(jnp.int32, (sub, sub), 1)
    return (col == jnp.asarray(src, jnp.int32)[:, None]).astype(BF16)


def _s5_lane_shuffle():
    gpt = LANES // S5_GROUP
    n = S5_CHUNK * LANES
    src = np.arange(n)
    step, grp, ch = src // LANES, (src % LANES) // S5_GROUP, src % S5_GROUP
    dst = grp * (S5_CHUNK * S5_GROUP) + step * S5_GROUP + ch
    assert gpt * S5_CHUNK * S5_GROUP == n
    col = lax.broadcasted_iota(jnp.int32, (n, n), 1)
    return (col == jnp.asarray(dst, jnp.int32)[:, None]).astype(BF16)


def _s5_tables(lam_re, lam_im, log_dt, b_re, b_im, c_re, c_im, n_chunks):
    t = S5_CHUNK
    n_groups, p = lam_re.shape[1:]
    gc = b_re.shape[-1]
    lr = jnp.minimum(lam_re.astype(F32), MIN_NEG_RE)
    li = lam_im.astype(F32)
    dt = jnp.exp(log_dt.astype(F32))[..., None]
    x, th = lr * dt, li * dt
    mag = jnp.exp(x)
    lb_re, lb_im = mag * jnp.cos(th), mag * jnp.sin(th)
    den = lr * lr + li * li
    nr, ni = lb_re - 1.0, lb_im
    f_re = (nr * lr + ni * li) / den
    f_im = (ni * lr - nr * li) / den
    brt = jnp.swapaxes(b_re.astype(F32), -1, -2)
    bit = jnp.swapaxes(b_im.astype(F32), -1, -2)
    bb_re = f_re[:, :, None] * brt - f_im[:, :, None] * bit
    bb_im = f_re[:, :, None] * bit + f_im[:, :, None] * brt
    cr, ci = c_re.astype(F32), c_im.astype(F32)

    kk = jnp.arange(t + 1, dtype=F32)[None, None, :, None, None]
    pmag = jnp.exp(kk * x[:, :, None, None, :])
    pw_re = pmag * jnp.cos(kk * th[:, :, None, None, :])
    pw_im = pmag * jnp.sin(kk * th[:, :, None, None, :])

    cp = jnp.concatenate([cr[:, :, None] * pw_re - ci[:, :, None] * pw_im,
                          -(cr[:, :, None] * pw_im + ci[:, :, None] * pw_re)], axis=-1)
    pb = jnp.concatenate([pw_re * bb_re[:, :, None] - pw_im * bb_im[:, :, None],
                          pw_re * bb_im[:, :, None] + pw_im * bb_re[:, :, None]], axis=-1)
    kern = jnp.einsum('dgkcp,dgep->dgkce', cp[:, :, :t], pb[:, :, 0], precision=HIGHEST)
    kern = jnp.concatenate([kern[0], kern[1]], axis=1)

    sig = np.arange(t)[:, None]
    tau = np.arange(t)[None, :]
    sel = np.zeros((2 * t, t, t), np.float32)
    for k in range(t):
        sel[k] = (tau - sig == k)
        sel[t + k] = (sig - tau == k)
    toep = jnp.einsum('kst,gkce->gsetc', jnp.asarray(sel), kern, precision=HIGHEST)
    toep = toep.reshape(n_groups, t * gc, t * gc)

    sin_f = pb[0][:, t - 1::-1][:, :t].reshape(n_groups, t * gc, 2 * p)
    sin_b = pb[1][:, :t].reshape(n_groups, t * gc, 2 * p)
    w1 = jnp.concatenate([toep, sin_f, sin_b], axis=2)
    out_f = cp[0][:, 1:t + 1].reshape(n_groups, t * gc, 2 * p)
    out_b = cp[1][:, t:0:-1].reshape(n_groups, t * gc, 2 * p)
    w2 = jnp.concatenate([out_f, out_b], axis=2)

    a_re, a_im = pw_re[:, :, t, 0], pw_im[:, :, t, 0]
    a1, a2 = [], []
    for _ in range(n_chunks.bit_length() - 1):
        a1.append(jnp.concatenate([a_re, a_re], axis=-1))
        a2.append(jnp.concatenate([-a_im, a_im], axis=-1))
        a_re, a_im = a_re * a_re - a_im * a_im, 2.0 * a_re * a_im
    a1 = jnp.transpose(jnp.stack(a1, axis=2), (1, 0, 2, 3))
    a2 = jnp.transpose(jnp.stack(a2, axis=2), (1, 0, 2, 3))
    return w1.astype(BF16), w2.astype(BF16), a1, a2


def _glu_ln_kernel(y_ref, x_ref, d_ref, wv_ref, wg_ref, g_ref, b_ref, o_ref, *, alpha):
    tm = o_ref.shape[0]
    sub = math.gcd(tm, EPILOGUE_SUB_ROWS)
    for r0 in range(0, tm, sub):
        x = x_ref[r0:r0 + sub, :]
        a = jax.nn.gelu(y_ref[r0:r0 + sub, :] + d_ref[...] * x).astype(BF16)
        val = jnp.dot(a, wv_ref[...], preferred_element_type=F32)
        gate = jnp.dot(a, wg_ref[...], preferred_element_type=F32)
        z = alpha * x + val * jax.nn.sigmoid(gate)
        o_ref[r0:r0 + sub, :] = _layer_norm_rows(z, g_ref[...], b_ref[...])


def _glu_ln(y, x, d_skip, wv, wg, g, b, *, alpha, tm):
    m, d = x.shape
    rows = pl.BlockSpec((tm, d), lambda i: (i, 0))
    whole = lambda a: pl.BlockSpec(a.shape, lambda i: (0,) * a.ndim)
    return pl.pallas_call(
        functools.partial(_glu_ln_kernel, alpha=alpha),
        grid=(m // tm,),
        in_specs=[rows, rows, whole(d_skip), whole(wv), whole(wg), whole(g), whole(b)],
        out_specs=rows,
        out_shape=jax.ShapeDtypeStruct((m, d), F32),
        compiler_params=_compiler_params(1),
        name="glu_ln",
    )(y, x, d_skip, wv, wg, g, b)


def _qkv_kernel(x_ref, w_ref, o_ref, *, d_q, tn, scale):
    xb = x_ref[...].astype(BF16)
    for c0 in range(0, w_ref.shape[1], tn):
        r = jnp.dot(xb, w_ref[:, c0:c0 + tn], preferred_element_type=F32)
        if c0 < d_q:
            r = r * scale
        o_ref[:, c0:c0 + tn] = r.astype(BF16)


def _qkv_proj(x, w, *, d_q, scale, tm, tn):
    m, d = x.shape
    n_out = w.shape[1]
    assert d_q % tn == 0 and n_out % tn == 0
    return pl.pallas_call(
        functools.partial(_qkv_kernel, d_q=d_q, tn=tn, scale=scale),
        grid=(m // tm,),
        in_specs=[
            pl.BlockSpec((tm, d), lambda i: (i, 0)),
            pl.BlockSpec(w.shape, lambda i: (0, 0)),
        ],
        out_specs=pl.BlockSpec((tm, n_out), lambda i: (i, 0)),
        out_shape=jax.ShapeDtypeStruct((m, n_out), BF16),
        compiler_params=_compiler_params(1),
        name="qkv_proj",
    )(x, w)


NATTEN_ROWS_PER_ITER = 32


def _natten_kernel(q_ref, k_ref, v_ref, bias_ref, o_ref, *, grid_rows, kh, rows_per_iter):
    w = GRID_W

    def row_block(rb, carry):
        pending = []
        for u in range(rows_per_iter):
            r = rb * rows_per_iter + u
            rs = jnp.clip(r - kh // 2, 0, grid_rows - kh)
            q = q_ref[0, pl.ds(pl.multiple_of(r * w, w), w), :]
            k = k_ref[0, pl.ds(pl.multiple_of(rs * w, w), kh * w), :]
            s = lax.dot_general(q, k, NT_DIMS, preferred_element_type=F32)
            pending.append((r, rs, s + bias_ref[0, rs - r + (WIN_H - 1)]))
        for r, rs, s in pending:
            v = v_ref[0, pl.ds(pl.multiple_of(rs * w, w), kh * w), :]
            p = jnp.exp(s - jnp.max(s, axis=-1, keepdims=True))
            denom = jnp.sum(p, axis=-1, keepdims=True)
            o = jnp.dot(p.astype(BF16), v, preferred_element_type=F32) / denom
            o_ref[0, pl.ds(pl.multiple_of(r * w, w), w), :] = o.astype(BF16)
        return carry

    lax.fori_loop(0, grid_rows // rows_per_iter, row_block, 0)


def _natten(qkv, bias, *, kh):
    bsz, seq, three_d = qkv.shape
    n_heads = bias.shape[0]
    dh = three_d // (3 * n_heads)
    grid_rows = seq // GRID_W
    rows_per_iter = math.gcd(grid_rows, NATTEN_ROWS_PER_ITER)
    return pl.pallas_call(
        functools.partial(_natten_kernel, grid_rows=grid_rows, kh=kh, rows_per_iter=rows_per_iter),
        grid=(n_heads, bsz),
        in_specs=[
            pl.BlockSpec((1, seq, dh), lambda h, b: (b, 0, h)),
            pl.BlockSpec((1, seq, dh), lambda h, b: (b, 0, n_heads + h)),
            pl.BlockSpec((1, seq, dh), lambda h, b: (b, 0, 2 * n_heads + h)),
            pl.BlockSpec((1,) + bias.shape[1:], lambda h, b: (h, 0, 0, 0)),
        ],
        out_specs=pl.BlockSpec((1, seq, dh), lambda h, b: (b, 0, h)),
        out_shape=jax.ShapeDtypeStruct((bsz, seq, n_heads * dh), BF16),
        compiler_params=_compiler_params(2),
        name="natten",
    )(qkv, qkv, qkv, bias)


def _natten_bias(rpb, kh):
    cols = np.arange(GRID_W)
    col_start = np.clip(cols - WIN_W // 2, 0, GRID_W - WIN_W)
    kc = np.arange(GRID_W)
    in_win = (kc[None, :] >= col_start[:, None]) & (kc[None, :] < col_start[:, None] + WIN_W)
    col_off = kc[None, :] - cols[:, None] + (WIN_W - 1)
    onehot = ((col_off[None] == np.arange(2 * WIN_W - 1)[:, None, None]) & in_win[None]).astype(np.float32)
    n_off = 2 * WIN_H - kh
    r2 = jnp.stack([rpb.astype(F32)[:, o:o + kh] for o in range(n_off)], axis=1)
    b = jnp.einsum('hriw,wck->hrcik', r2, jnp.asarray(onehot), precision=HIGHEST)
    b = b + jnp.asarray(np.where(in_win, 0.0, MASK_VALUE), F32)[None, None, :, None, :]
    return b.reshape(b.shape[0], n_off, GRID_W, kh * GRID_W)


def _proj_ln_kernel(a_ref, x_ref, w_ref, g_ref, b_ref, o_ref, *, alpha):
    tm = o_ref.shape[0]
    sub = math.gcd(tm, EPILOGUE_SUB_ROWS)
    for r0 in range(0, tm, sub):
        z = alpha * x_ref[r0:r0 + sub, :] + jnp.dot(a_ref[r0:r0 + sub, :], w_ref[...], preferred_element_type=F32)
        o_ref[r0:r0 + sub, :] = _layer_norm_rows(z, g_ref[...], b_ref[...])


def _proj_ln(a, x, w, g, b, *, alpha, tm):
    m, d = x.shape
    whole = lambda v: pl.BlockSpec(v.shape, lambda i: (0,) * v.ndim)
    return pl.pallas_call(
        functools.partial(_proj_ln_kernel, alpha=alpha),
        grid=(m // tm,),
        in_specs=[pl.BlockSpec((tm, a.shape[1]), lambda i: (i, 0)), pl.BlockSpec((tm, d), lambda i: (i, 0)),
                  whole(w), whole(g), whole(b)],
        out_specs=pl.BlockSpec((tm, d), lambda i: (i, 0)),
        out_shape=jax.ShapeDtypeStruct((m, d), F32),
        compiler_params=_compiler_params(1),
        name="proj_ln",
    )(a, x, w, g, b)


def _tiles(m, d, bsz, seq):
    def rows(pref):
        return pref if m % pref == 0 else m
    s5_seqs = max(1, min(bsz, S5_TOKENS_PER_STEP // seq))
    while bsz % s5_seqs:
        s5_seqs -= 1
    return dict(tm=rows(512), ffn_tm=rows(1024), ffn_tf=512, qkv_tm=rows(512), qkv_tn=d,
                s5_tokens=s5_seqs * seq, s5_sub=math.gcd(seq, S5_SUB_TOKENS))


def _cast_kernel(w_ref, o_ref, *, scale):
    w = w_ref[...]
    o_ref[...] = (w if scale is None else w * scale).astype(BF16)


def _cast_bf16(w, *, tr, scale=None):
    a, b, r, c = w.shape
    assert r % tr == 0
    spec = pl.BlockSpec((None, None, tr, c), lambda i, j, k: (i, j, k, 0))
    return pl.pallas_call(
        functools.partial(_cast_kernel, scale=scale),
        grid=(a, b, r // tr),
        in_specs=[spec],
        out_specs=spec,
        out_shape=jax.ShapeDtypeStruct(w.shape, BF16),
        compiler_params=_compiler_params(3),
        name="cast_bf16",
    )(w)


def _row_block(r, target):
    return max(t for t in range(16, target + 1, 16) if r % t == 0)


def kernel(x, ffn_w_gate, ffn_w_up, ffn_w_down, ln_g, ln_b, s5_lam_re, s5_lam_im, s5_log_dt, s5_b_re, s5_b_im, s5_c_re, s5_c_im, s5_d, s5_w_glu_val, s5_w_glu_gate, na_w_qkv, na_rpb, na_w_out):
    bsz, seq, d = x.shape
    depth = ffn_w_gate.shape[0]
    alpha = (2 * depth) ** 0.25
    m = bsz * seq
    n_chunks = seq // S5_CHUNK
    t = _tiles(m, d, bsz, seq)
    tm = t["tm"]
    grid_rows = seq // GRID_W
    kh = min(WIN_H, grid_rows)
    dh = d // N_HEADS

    wg_all = _cast_bf16(ffn_w_gate, tr=_row_block(d, 256))
    wu_all = _cast_bf16(ffn_w_up, tr=_row_block(d, 256))
    wd_all = _cast_bf16(ffn_w_down, tr=_row_block(ffn_w_down.shape[2], 768), scale=0.5)
    g_all = ln_g[:, :, None, :]
    b_all = ln_b[:, :, None, :]

    def ffn(h, i, k):
        return _ffn_ln(h, wg_all, wu_all, wd_all, g_all, b_all, i, k, alpha=alpha, tm=t["ffn_tm"], tf=t["ffn_tf"])

    h = x.reshape(m, d)
    for i in range(depth):
        h = ffn(h, i, 0)
        jm = i // N_MIXERS
        g1, b1 = ln_g[i, 1][None], ln_b[i, 1][None]
        if i % N_MIXERS == 0:
            w1, w2, a1, a2 = _s5_tables(s5_lam_re[jm], s5_lam_im[jm], s5_log_dt[jm], s5_b_re[jm], s5_b_im[jm],
                                        s5_c_re[jm], s5_c_im[jm], n_chunks)
            y = _s5_ssm(h, _s5_row_shuffle(t["s5_sub"]), _s5_lane_shuffle(), w1, w2, a1, a2, n_chunks=n_chunks,
                        tokens=t["s5_tokens"])
            h = _glu_ln(y, h, s5_d[jm][None], s5_w_glu_val[jm].astype(BF16), s5_w_glu_gate[jm].astype(BF16),
                        g1, b1, alpha=alpha, tm=tm)
        else:
            qkv = _qkv_proj(h, na_w_qkv[jm].astype(BF16), d_q=d, scale=dh ** -0.5, tm=t["qkv_tm"], tn=t["qkv_tn"])
            att = _natten(qkv.reshape(bsz, seq, 3 * d), _natten_bias(na_rpb[jm], kh), kh=kh)
            h = _proj_ln(att.reshape(m, d), h, na_w_out[jm].astype(BF16), g1, b1, alpha=alpha, tm=tm)
        h = ffn(h, i, 1)
    return h.reshape(bsz, seq, d)
```
